```python
import jax, jax.numpy as jnp
from jax import lax
import numpy as np

D_MODEL = 2048
BATCH = 4
SEQ = 8192
DEPTH = 1

N_HEADS = 8
N_KV_HEADS = 2
HEAD_DIM = 128
ATTN_WIDTH = N_HEADS * HEAD_DIM
KV_WIDTH = N_KV_HEADS * HEAD_DIM
ROPE_THETA = 500000.0
ROPE_FRACTION = 4
IDX_HEADS = 16
IDX_DIM = 64
TOPK_MAX = 256
Q_BLOCK = 128
POOL_GROUPS = 4
POOL_WINDOWS = (2, 4, 8, 16)
POOL_WIDTH = D_MODEL - ATTN_WIDTH
POOL_GROUP_DIM = POOL_WIDTH // POOL_GROUPS
MIX_WIDTH = ATTN_WIDTH + POOL_WIDTH
N_EXPERT_GROUPS = 4
EXPERTS_PER_GROUP = 8
N_EXPERTS = N_EXPERT_GROUPS * EXPERTS_PER_GROUP
EXPERT_TOPK = 2
D_EXPERT = 512
NORM_EPS = 1e-6
PROJ_SPLITS = (ATTN_WIDTH, KV_WIDTH, KV_WIDTH, IDX_HEADS * IDX_DIM, IDX_DIM, IDX_HEADS, POOL_WIDTH)
PROJ_WIDTH = sum(PROJ_SPLITS)

kernel_name = "hymba_dsa_pool_hiermoe_adaln"


def rmsnorm(x, g):
    xf = x.astype(jnp.float32)
    y = xf * lax.rsqrt(jnp.mean(xf * xf, axis=-1, keepdims=True) + NORM_EPS)
    return y.astype(x.dtype) * g


def modulate(xn, shift, scale):
    return xn * (1.0 + scale[:, None, :]) + shift[:, None, :]


def rope_partial(x, pos):
    d = x.shape[-1]
    rd = d // ROPE_FRACTION
    half = rd // 2
    inv = jnp.float32(ROPE_THETA) ** (-(jnp.arange(half, dtype=jnp.float32) * 2.0) / rd)
    ang = pos[:, None] * inv[None, :]
    cos = jnp.cos(ang)[None, :, None, :]
    sin = jnp.sin(ang)[None, :, None, :]
    xr = x[..., :rd].astype(jnp.float32)
    x1, x2 = xr[..., :half], xr[..., half:]
    rot = jnp.concatenate([x1 * cos - x2 * sin, x2 * cos + x1 * sin], axis=-1).astype(x.dtype)
    return jnp.concatenate([rot, x[..., rd:]], axis=-1)


def dsa_attention(q, k, v, qi, ki, wi):
    B, S = q.shape[0], q.shape[1]
    n_sel = min(TOPK_MAX, S // 4)
    nb = S // Q_BLOCK
    rep = N_HEADS // N_KV_HEADS

    def to_blocks(a):
        return jnp.moveaxis(a.reshape((B, nb, Q_BLOCK) + a.shape[2:]), 1, 0)

    t_blocks = jnp.arange(S, dtype=jnp.int32).reshape(nb, Q_BLOCK)
    s_pos = jnp.arange(S, dtype=jnp.int32)
    b_ix = jnp.arange(B)[:, None, None]
    idx_scale = (IDX_DIM ** -0.5) * (IDX_HEADS ** -0.5)
    attn_scale = HEAD_DIM ** -0.5
    kif = ki.astype(jnp.float32)

    def one_block(args):
        qb, qib, wib, tb = args
        logits = jnp.einsum('bthd,bsd->bths', qib.astype(jnp.float32), kif)
        score = jnp.einsum('bths,bth->bts', jax.nn.relu(logits), wib.astype(jnp.float32)) * idx_scale
        causal = s_pos[None, :] <= tb[:, None]
        score = jnp.where(causal[None], score, -jnp.inf)
        _, sel = lax.top_k(score, n_sel)
        valid = sel <= tb[None, :, None]
        kg = k[b_ix, sel].astype(jnp.float32)
        vg = v[b_ix, sel].astype(jnp.float32)
        qg = qb.reshape(B, Q_BLOCK, N_KV_HEADS, rep, HEAD_DIM).astype(jnp.float32)
        s = jnp.einsum('btgrd,btkgd->btgrk', qg, kg) * attn_scale
        s = jnp.where(valid[:, :, None, None, :], s, -jnp.inf)
        p = jax.nn.softmax(s, axis=-1)
        o = jnp.einsum('btgrk,btkgd->btgrd', p, vg).astype(qb.dtype)
        return o.reshape(B, Q_BLOCK, ATTN_WIDTH)

    out = lax.map(one_block, (to_blocks(q), to_blocks(qi), to_blocks(wi), t_blocks))
    return jnp.moveaxis(out, 0, 1).reshape(B, S, ATTN_WIDTH)


def multiscale_pool(p, w_pool, pool_scale):
    B, S = p.shape[0], p.shape[1]
    pg = p.reshape(B, S, POOL_GROUPS, POOL_GROUP_DIM).astype(jnp.float32)
    cs = jnp.cumsum(pg, axis=1)
    cs_pad = jnp.pad(cs, ((0, 0), (1, 0), (0, 0), (0, 0)))
    win = jnp.array(POOL_WINDOWS, dtype=jnp.int32)
    t = jnp.arange(S, dtype=jnp.int32)[:, None]
    lo = jnp.maximum(t + 1 - win[None, :], 0)
    lag = cs_pad[:, lo, jnp.arange(POOL_GROUPS)[None, :], :]
    count = jnp.minimum(t + 1, win[None, :]).astype(jnp.float32)
    mixed = (cs - lag) / count[None, :, :, None] - pg
    y = jnp.einsum('bsgc,gcd->bsgd', mixed.astype(p.dtype), w_pool) * pool_scale
    return y.reshape(B, S, POOL_WIDTH)


def hier_moe(h, w_grp, w_exp, w1, w3, w2):
    B, S, D = h.shape
    ht = h.reshape(-1, D)
    pg = jax.nn.softmax((ht @ w_grp).astype(jnp.float32), axis=-1)
    g_sel = jnp.argmax(pg, axis=-1)
    p_g = jnp.max(pg, axis=-1)
    el = (ht @ w_exp).astype(jnp.float32).reshape(-1, N_EXPERT_GROUPS, EXPERTS_PER_GROUP)
    el_sel = jnp.take_along_axis(el, g_sel[:, None, None], axis=1)[:, 0]
    pe = jax.nn.softmax(el_sel, axis=-1)
    top_p, top_i = lax.top_k(pe, EXPERT_TOPK)
    top_w = top_p / jnp.sum(top_p, axis=-1, keepdims=True) * p_g[:, None]
    eid = g_sel[:, None] * EXPERTS_PER_GROUP + top_i
    combine = jnp.einsum('tke,tk->te', jax.nn.one_hot(eid, N_EXPERTS, dtype=jnp.float32), top_w)
    y = jnp.zeros((ht.shape[0], D), jnp.float32)
    for e in range(N_EXPERTS):
        a = jax.nn.silu(ht @ w1[e]) * (ht @ w3[e])
        y = y + combine[:, e:e + 1] * (a @ w2[e]).astype(jnp.float32)
    return y.astype(h.dtype).reshape(B, S, D)


def setup_inputs(seed: int = 0) -> dict:
    key = jax.random.key(seed)
    ks = jax.random.split(key, 20)
    f32 = jnp.float32
    L, D = DEPTH, D_MODEL
    nrm = lambda k, shape, s: jax.random.normal(k, shape, f32) * s
    return {
        "x": nrm(ks[0], (BATCH, SEQ, D), 1.0),
        "c": nrm(ks[1], (BATCH, D), 1.0),
        "w_ada": nrm(ks[2], (L, D, 6 * D), 0.5 * D ** -0.5),
        "b_ada": nrm(ks[3], (L, 6 * D), 0.02),
        "norm1_g": 1.0 + nrm(ks[4], (L, D), 0.02),
        "w_in": nrm(ks[5], (L, D, PROJ_WIDTH), D ** -0.5),
        "q_norm_g": 1.0 + nrm(ks[6], (L, HEAD_DIM), 0.02),
        "k_norm_g": 1.0 + nrm(ks[7], (L, HEAD_DIM), 0.02),
        "w_pool": nrm(ks[8], (L, POOL_GROUPS, POOL_GROUP_DIM, POOL_GROUP_DIM), POOL_GROUP_DIM ** -0.5),
        "pool_scale": 1.0 + nrm(ks[9], (L, POOL_GROUPS, POOL_GROUP_DIM), 0.02),
        "w_out": nrm(ks[10], (L, MIX_WIDTH, D), MIX_WIDTH ** -0.5),
        "norm2_g": 1.0 + nrm(ks[11], (L, D), 0.02),
        "w_grp": nrm(ks[12], (L, D, N_EXPERT_GROUPS), D ** -0.5),
        "w_exp": nrm(ks[13], (L, D, N_EXPERTS), D ** -0.5),
        "w1": nrm(ks[14], (L, N_EXPERTS, D, D_EXPERT), D ** -0.5),
        "w3": nrm(ks[15], (L, N_EXPERTS, D, D_EXPERT), D ** -0.5),
        "w2": nrm(ks[16], (L, N_EXPERTS, D_EXPERT, D), D_EXPERT ** -0.5),
    }


def reference(x, c, w_ada, b_ada, norm1_g, w_in, q_norm_g, k_norm_g, w_pool, pool_scale,
              w_out, norm2_g, w_grp, w_exp, w1, w3, w2):
    B, S, D = x.shape
    pos = jnp.arange(S, dtype=jnp.float32)
    offsets = [int(o) for o in np.cumsum(PROJ_SPLITS)[:-1]]
    for l in range(DEPTH):
        ada = jax.nn.silu(c) @ w_ada[l] + b_ada[l]
        sh1, sc1, gt1, sh2, sc2, gt2 = jnp.split(ada, 6, axis=-1)
        h = modulate(rmsnorm(x, norm1_g[l]), sh1, sc1)
        proj = h @ w_in[l]
        q, k, v, qi, ki, wi, p = jnp.split(proj, offsets, axis=-1)
        q = rope_partial(rmsnorm(q.reshape(B, S, N_HEADS, HEAD_DIM), q_norm_g[l]), pos)
        k = rope_partial(rmsnorm(k.reshape(B, S, N_KV_HEADS, HEAD_DIM), k_norm_g[l]), pos)
        v = v.reshape(B, S, N_KV_HEADS, HEAD_DIM)
        qi = rope_partial(qi.reshape(B, S, IDX_HEADS, IDX_DIM), pos)
        ki = rope_partial(ki.reshape(B, S, 1, IDX_DIM), pos)[:, :, 0]
        attn_out = dsa_attention(q, k, v, qi, ki, wi)
        pool_out = multiscale_pool(p, w_pool[l], pool_scale[l])
        mix = jnp.concatenate([attn_out, pool_out], axis=-1) @ w_out[l]
        x = x + gt1[:, None, :] * mix
        h2 = modulate(rmsnorm(x, norm2_g[l]), sh2, sc2)
        x = x + gt2[:, None, :] * hier_moe(h2, w_grp[l], w_exp[l], w1[l], w3[l], w2[l])
    return x
```

```python
import functools

import numpy as np
import jax
import jax.numpy as jnp
from jax import lax
from jax.experimental import pallas as pl
from jax.experimental.pallas import tpu as pltpu

N_HEADS = 8
N_KV_HEADS = 2
HEAD_DIM = 128
ATTN_WIDTH = N_HEADS * HEAD_DIM
KV_WIDTH = N_KV_HEADS * HEAD_DIM
ROPE_THETA = 500000.0
ROPE_FRACTION = 4
IDX_HEADS = 16
IDX_DIM = 64
TOPK_MAX = 256
POOL_GROUPS = 4
POOL_WINDOWS = (2, 4, 8, 16)
POOL_GROUP_DIM = 256
POOL_WIDTH = POOL_GROUPS * POOL_GROUP_DIM
N_EXPERT_GROUPS = 4
EXPERTS_PER_GROUP = 8
N_EXPERTS = N_EXPERT_GROUPS * EXPERTS_PER_GROUP
D_EXPERT = 512
NORM_EPS = 1e-6

LANES = 128
VMEM_LIMIT_BYTES = 56 * 1024 * 1024

KEY_TILE = 512
Q_TILE = 128
POOL_HALO = 16
EXPERT_TILE = 256
ROW_TILE = 256

_C_Q = 0
_C_K = _C_Q + ATTN_WIDTH
_C_V = _C_K + KV_WIDTH
_C_QI = _C_V + KV_WIDTH
_C_KI = _C_QI + IDX_HEADS * IDX_DIM
_C_WI = _C_KI + LANES
_C_P = _C_WI + LANES
_C_END = _C_P + POOL_WIDTH

_NEG = -1e30
_INT_MIN = -2 ** 31
_KEY_LOWEST_FINITE = -2 ** 31 + 0x00800000


def _cparams(semantics):
    return pltpu.CompilerParams(dimension_semantics=semantics, vmem_limit_bytes=VMEM_LIMIT_BYTES)


def _ada_kernel(c_ref, w_ref, b_ref, o_ref):
    c = c_ref[...]
    s = c * (1.0 / (1.0 + jnp.exp(-c)))
    o_ref[...] = jnp.dot(s, w_ref[...], preferred_element_type=jnp.float32,
                         precision=lax.Precision.HIGHEST) + b_ref[...]


def _ada(c, w_ada, b_ada):
    B, D = c.shape
    N = w_ada.shape[1]
    tn = 1024
    rows = 8
    c_pad = jnp.zeros((rows, D), jnp.float32).at[:B].set(c)
    out = pl.pallas_call(
        _ada_kernel,
        grid=(N // tn,),
        in_specs=[pl.BlockSpec((rows, D), lambda j: (0, 0)),
                  pl.BlockSpec((D, tn), lambda j: (0, j)),
                  pl.BlockSpec((1, tn), lambda j: (0, j))],
        out_specs=pl.BlockSpec((rows, tn), lambda j: (0, j)),
        out_shape=jax.ShapeDtypeStruct((rows, N), jnp.float32),
        compiler_params=_cparams(("arbitrary",)),
        name="ada",
    )(c_pad, w_ada, b_ada.reshape(1, N))
    return out[:B]


def _rope_tables(S, head_dim):
    rd = head_dim // ROPE_FRACTION
    half = rd // 2
    pos = jnp.arange(S, dtype=jnp.float32)
    inv = jnp.float32(ROPE_THETA) ** (-(jnp.arange(half, dtype=jnp.float32) * 2.0) / rd)
    ang = pos[:, None] * inv[None, :]
    cos, sin = jnp.cos(ang), jnp.sin(ang)
    lane = np.arange(LANES) % head_dim
    fidx = np.where(lane < half, lane, lane - half) % half
    in_lo = jnp.asarray(lane < half)[None, :]
    in_hi = jnp.asarray((lane >= half) & (lane < rd))[None, :]
    cos_l, sin_l = cos[:, fidx], sin[:, fidx]
    c_tab = jnp.where(in_lo | in_hi, cos_l, 1.0)
    s1_tab = jnp.where(in_lo, -sin_l, 0.0)
    s2_tab = jnp.where(in_hi, sin_l, 0.0)
    return c_tab, s1_tab, s2_tab, half


def _rope(x, c_tab, s1_tab, s2_tab, half):
    return (x * c_tab + pltpu.roll(x, LANES - half, 1) * s1_tab + pltpu.roll(x, half, 1) * s2_tab)


def _inproj_kernel(x_ref, ada_ref, g1_ref, w_ref, qg_ref, kg_ref,
                   cq_ref, s1q_ref, s2q_ref, ci_ref, s1i_ref, s2i_ref,
                   q_ref, kt_ref, v_ref, qi_ref, kia_ref, kib_ref, wi_ref, p_ref,
                   *, half_qk, half_idx, n_chunks):
    x = x_ref[...]
    ms = jnp.mean(x * x, axis=-1, keepdims=True)
    xn = x * lax.rsqrt(ms + NORM_EPS) * g1_ref[...]
    h = (xn * (1.0 + ada_ref[0, 1:2, :]) + ada_ref[0, 0:1, :]).astype(jnp.bfloat16)

    def proj(lo, hi):
        return jnp.dot(h, w_ref[:, lo:hi], preferred_element_type=jnp.float32)

    cq, s1q, s2q = cq_ref[...], s1q_ref[...], s2q_ref[...]
    ci, s1i, s2i = ci_ref[...], s1i_ref[...], s2i_ref[...]

    def qk_head(slab, gain):
        m = jnp.mean(slab * slab, axis=-1, keepdims=True)
        y = slab * lax.rsqrt(m + NORM_EPS) * gain
        return _rope(y, cq, s1q, s2q, half_qk)

    attn_scale = HEAD_DIM ** -0.5
    q = proj(_C_Q, _C_K)
    for hd in range(N_HEADS):
        sl = slice(hd * HEAD_DIM, (hd + 1) * HEAD_DIM)
        q_ref[:, sl] = (qk_head(q[:, sl], qg_ref[...]) * attn_scale).astype(jnp.bfloat16)

    k = proj(_C_K, _C_V)
    for g in range(N_KV_HEADS):
        kt = qk_head(k[:, g * HEAD_DIM:(g + 1) * HEAD_DIM], kg_ref[...]).T.astype(jnp.bfloat16)
        for c in range(n_chunks):
            kt_ref[0, g, c] = kt[:, c * KEY_TILE:(c + 1) * KEY_TILE]

    v_ref[...] = proj(_C_V, _C_QI).astype(jnp.bfloat16)

    qi = proj(_C_QI, _C_KI)
    for j in range(IDX_HEADS * IDX_DIM // LANES):
        sl = slice(j * LANES, (j + 1) * LANES)
        qi_ref[:, sl] = _rope(qi[:, sl], ci, s1i, s2i, half_idx).astype(jnp.bfloat16)

    ki = _rope(proj(_C_KI, _C_WI), ci, s1i, s2i, half_idx).T
    ki_swapped = jnp.concatenate([ki[IDX_DIM:], ki[:IDX_DIM]], axis=0)
    for c in range(n_chunks):
        kia_ref[0, c] = ki[:, c * KEY_TILE:(c + 1) * KEY_TILE].astype(jnp.bfloat16)
        kib_ref[0, c] = ki_swapped[:, c * KEY_TILE:(c + 1) * KEY_TILE].astype(jnp.bfloat16)

    idx_scale = (IDX_DIM ** -0.5) * (IDX_HEADS ** -0.5)
    wi_ref[...] = proj(_C_WI, _C_P) * idx_scale
    p_ref[...] = proj(_C_P, _C_END)


def _inproj(x2d, ada3, norm1_g, w_in, q_norm_g, k_norm_g, B, S):
    T, D = x2d.shape
    tm = min(512, S)
    n_chunks = tm // KEY_TILE
    tiles_per_seq = S // tm
    bf = jnp.bfloat16

    offs = np.cumsum((ATTN_WIDTH, KV_WIDTH, KV_WIDTH, IDX_HEADS * IDX_DIM, IDX_DIM, IDX_HEADS))
    o_q, o_k, o_v, o_qi, o_ki, o_wi = [int(o) for o in offs]
    zeros = lambda n: jnp.zeros((D, n), w_in.dtype)
    w_pad = jnp.concatenate([
        w_in[:, :o_qi],
        w_in[:, o_qi:o_ki], zeros(LANES - IDX_DIM),
        w_in[:, o_ki:o_wi], zeros(LANES - IDX_HEADS),
        w_in[:, o_wi:],
    ], axis=1).astype(bf)
    assert w_pad.shape[1] == _C_END

    cq, s1q, s2q, half_qk = _rope_tables(S, HEAD_DIM)
    ci, s1i, s2i, half_idx = _rope_tables(S, IDX_DIM)

    row = lambda w: pl.BlockSpec((tm, w), lambda i: (i, 0))
    tab = pl.BlockSpec((tm, LANES), lambda i: (i % tiles_per_seq, 0))
    const = lambda shape: pl.BlockSpec(shape, lambda i: (0,) * len(shape))
    kern = functools.partial(_inproj_kernel, half_qk=half_qk, half_idx=half_idx, n_chunks=n_chunks)
    return pl.pallas_call(
        kern,
        grid=(T // tm,),
        in_specs=[row(D),
                  pl.BlockSpec((1, 6, D), lambda i: (i // tiles_per_seq, 0, 0)),
                  const((1, D)),
                  pl.BlockSpec((D, _C_END), lambda i: (0, 0), pipeline_mode=pl.Buffered(1)),
                  const((1, HEAD_DIM)), const((1, HEAD_DIM)),
                  tab, tab, tab, tab, tab, tab],
        out_specs=[row(ATTN_WIDTH),
                   pl.BlockSpec((1, N_KV_HEADS, n_chunks, HEAD_DIM, KEY_TILE),
                                lambda i: (i // tiles_per_seq, 0, i % tiles_per_seq, 0, 0)),
                   row(KV_WIDTH),
                   row(IDX_HEADS * IDX_DIM),
                   pl.BlockSpec((1, n_chunks, LANES, KEY_TILE),
                                lambda i: (i // tiles_per_seq, i % tiles_per_seq, 0, 0)),
                   pl.BlockSpec((1, n_chunks, LANES, KEY_TILE),
                                lambda i: (i // tiles_per_seq, i % tiles_per_seq, 0, 0)),
                   row(LANES),
                   row(POOL_WIDTH)],
        out_shape=[jax.ShapeDtypeStruct((T, ATTN_WIDTH), bf),
                   jax.ShapeDtypeStruct((B, N_KV_HEADS, S // KEY_TILE, HEAD_DIM, KEY_TILE), bf),
                   jax.ShapeDtypeStruct((T, KV_WIDTH), bf),
                   jax.ShapeDtypeStruct((T, IDX_HEADS * IDX_DIM), bf),
                   jax.ShapeDtypeStruct((B, S // KEY_TILE, LANES, KEY_TILE), bf),
                   jax.ShapeDtypeStruct((B, S // KEY_TILE, LANES, KEY_TILE), bf),
                   jax.ShapeDtypeStruct((T, LANES), jnp.float32),
                   jax.ShapeDtypeStruct((T, POOL_WIDTH), jnp.float32)],
        compiler_params=_cparams(("arbitrary",)),
        name="inproj",
    )(x2d, ada3, norm1_g.reshape(1, D), w_pad, q_norm_g.reshape(1, HEAD_DIM), k_norm_g.reshape(1, HEAD_DIM),
      cq, s1q, s2q, ci, s1i, s2i)


def _dsa_kernel(q_ref, qi_ref, wi_ref, kia_ref, kib_ref, kt_ref, v_ref, o_ref,
                keys_ref, wb_ref, m_ref, l_ref, acc_ref, *, tq, n_sel):
    qb = pl.program_id(1)
    n_tiles = (qb * tq + tq + KEY_TILE - 1) // KEY_TILE
    lane_chunks = KEY_TILE // LANES
    rep = N_HEADS // N_KV_HEADS
    f32 = jnp.float32

    qi = qi_ref[...]
    n_slabs = IDX_HEADS * IDX_DIM // LANES
    lhs = jnp.concatenate([qi[:, j * LANES:(j + 1) * LANES] for j in range(n_slabs)], axis=0)
    w = wi_ref[...]
    for hd in range(IDX_HEADS):
        wb_ref[hd] = jnp.broadcast_to(w[:, hd:hd + 1], (tq, LANES))
    t_pos = qb * tq + lax.broadcasted_iota(jnp.int32, (tq, KEY_TILE), 0)
    k_off = lax.broadcasted_iota(jnp.int32, (tq, KEY_TILE), 1)

    def score_tile(j, carry):
        za = jnp.dot(lhs, kia_ref[0, j], preferred_element_type=f32)
        zb = jnp.dot(lhs, kib_ref[0, j], preferred_element_type=f32)
        acc = jnp.zeros((tq, KEY_TILE), f32)
        for s in range(n_slabs):
            wa = jnp.concatenate([wb_ref[2 * s]] * lane_chunks, axis=1)
            wo = jnp.concatenate([wb_ref[2 * s + 1]] * lane_chunks, axis=1)
            acc = acc + wa * jnp.maximum(za[s * tq:(s + 1) * tq], 0.0)
            acc = acc + wo * jnp.maximum(zb[s * tq:(s + 1) * tq], 0.0)
        score = jnp.where(j * KEY_TILE + k_off <= t_pos, acc, -jnp.inf)
        bits = lax.bitcast_convert_type(score, jnp.int32)
        keys_ref[j] = bits ^ ((bits >> 31) & 0x7FFFFFFF)
        return carry

    lax.fori_loop(0, n_tiles, score_tile, 0)

    def count_ge(cand):
        def body(j, cnt):
            kk = keys_ref[j]
            for cc in range(lane_chunks):
                cnt = cnt + jnp.where(kk[:, cc * LANES:(cc + 1) * LANES] >= cand, 1, 0)
            return cnt
        cnt = lax.fori_loop(0, n_tiles, body, jnp.zeros((tq, LANES), jnp.int32))
        return jnp.sum(cnt, axis=-1, keepdims=True)

    def bisect(i, thr):
        cand = thr + lax.shift_left(jnp.int32(1), 31 - i)
        return jnp.where(count_ge(cand) >= n_sel, cand, thr)

    thr = lax.fori_loop(0, 32, bisect, jnp.full((tq, LANES), _INT_MIN, jnp.int32))
    thr = jnp.maximum(thr, _KEY_LOWEST_FINITE)
    thr_wide = jnp.concatenate([thr] * lane_chunks, axis=1)

    q = q_ref[...]
    for g in range(N_KV_HEADS):
        qg = jnp.concatenate([q[:, (g * rep + r) * HEAD_DIM:(g * rep + r + 1) * HEAD_DIM] for r in range(rep)],
                             axis=0)
        m_ref[...] = jnp.full(m_ref.shape, _NEG, f32)
        l_ref[...] = jnp.zeros(l_ref.shape, f32)
        acc_ref[...] = jnp.zeros(acc_ref.shape, f32)

        def attend(j, carry, qg=qg, g=g):
            s = jnp.dot(qg, kt_ref[0, g, j], preferred_element_type=f32)
            bias = jnp.where(keys_ref[j] >= thr_wide, 0.0, _NEG)
            s = s + jnp.concatenate([bias] * rep, axis=0)
            m_old = m_ref[...]
            m_new = jnp.maximum(m_old, jnp.max(s, axis=-1, keepdims=True))
            alpha = jnp.exp(m_old - m_new)
            p = jnp.exp(s - m_new[:, :1])
            l_ref[...] = alpha * l_ref[...] + jnp.sum(p, axis=-1, keepdims=True)
            start = pl.multiple_of(j * KEY_TILE, KEY_TILE)
            vt = v_ref[pl.ds(start, KEY_TILE), g * HEAD_DIM:(g + 1) * HEAD_DIM]
            acc_ref[...] = alpha * acc_ref[...] + jnp.dot(p.astype(jnp.bfloat16), vt, preferred_element_type=f32)
            m_ref[...] = m_new
            return carry

        lax.fori_loop(0, n_tiles, attend, 0)
        o = acc_ref[...] / l_ref[...]
        for r in range(rep):
            hd = g * rep + r
            o_ref[:, hd * HEAD_DIM:(hd + 1) * HEAD_DIM] = o[r * tq:(r + 1) * tq].astype(o_ref.dtype)


def _dsa(q, qi, wi, kia, kib, kt, v, B, S):
    tq = min(Q_TILE, S)
    n_sel = min(TOPK_MAX, S // 4)
    nq = S // tq
    n_kt = S // KEY_TILE
    rep = N_HEADS // N_KV_HEADS
    row = lambda w: pl.BlockSpec((tq, w), lambda b, i: (b * nq + i, 0))
    once = pl.Buffered(1)
    kern = functools.partial(_dsa_kernel, tq=tq, n_sel=n_sel)
    return pl.pallas_call(
        kern,
        grid=(B, nq),
        in_specs=[row(ATTN_WIDTH), row(IDX_HEADS * IDX_DIM), row(LANES),
                  pl.BlockSpec((1, n_kt, LANES, KEY_TILE), lambda b, i: (b, 0, 0, 0), pipeline_mode=once),
                  pl.BlockSpec((1, n_kt, LANES, KEY_TILE), lambda b, i: (b, 0, 0, 0), pipeline_mode=once),
                  pl.BlockSpec((1, N_KV_HEADS, n_kt, HEAD_DIM, KEY_TILE), lambda b, i: (b, 0, 0, 0, 0),
                               pipeline_mode=once),
                  pl.BlockSpec((S, KV_WIDTH), lambda b, i: (b, 0), pipeline_mode=once)],
        out_specs=row(ATTN_WIDTH),
        out_shape=jax.ShapeDtypeStruct((B * S, ATTN_WIDTH), jnp.bfloat16),
        scratch_shapes=[pltpu.VMEM((n_kt, tq, KEY_TILE), jnp.int32),
                        pltpu.VMEM((IDX_HEADS, tq, LANES), jnp.float32),
                        pltpu.VMEM((rep * tq, LANES), jnp.float32),
                        pltpu.VMEM((rep * tq, LANES), jnp.float32),
                        pltpu.VMEM((rep * tq, HEAD_DIM), jnp.float32)],
        compiler_params=_cparams(("arbitrary", "arbitrary")),
        name="dsa",
    )(q, qi, wi, kia, kib, kt, v)


_META_EID, _META_W, _META_RANK = 0, 2, 4


def _mix_kernel(attn_ref, p_ref, halo_ref, x_ref, ada_ref, wpool_ref, pscale_ref, wout_ref, g2_ref, wr_ref,
                x1_ref, h2_ref, meta_ref, counts_ref, pbuf_ref, run_ref, *, tm, tiles_per_seq):
    i = pl.program_id(0)
    f32 = jnp.float32
    bf = jnp.bfloat16
    seq_tile = i % tiles_per_seq

    @pl.when(i == 0)
    def _():
        run_ref[...] = jnp.zeros(run_ref.shape, f32)

    p = p_ref[...]
    pbuf_ref[0:POOL_HALO, :] = jnp.where(seq_tile == 0, 0.0, halo_ref[...])
    pbuf_ref[POOL_HALO:POOL_HALO + tm, :] = p
    t_pos = seq_tile * tm + lax.broadcasted_iota(jnp.int32, (tm, 1), 0)
    pieces = [attn_ref[...]]
    for g, win in enumerate(POOL_WINDOWS):
        cs = slice(g * POOL_GROUP_DIM, (g + 1) * POOL_GROUP_DIM)
        tot = p[:, cs]
        for back in range(1, win):
            tot = tot + pbuf_ref[POOL_HALO - back:POOL_HALO - back + tm, cs]
        count = jnp.minimum(t_pos + 1, win).astype(f32)
        mixed = tot / count - p[:, cs]
        y = jnp.dot(mixed.astype(bf), wpool_ref[g], preferred_element_type=f32) * pscale_ref[:, cs]
        pieces.append(y.astype(bf))
    mix = jnp.dot(jnp.concatenate(pieces, axis=1), wout_ref[...], preferred_element_type=f32)
    x1 = x_ref[...] + ada_ref[0, 2:3, :] * mix
    x1_ref[...] = x1

    ms = jnp.mean(x1 * x1, axis=-1, keepdims=True)
    h2 = (x1 * lax.rsqrt(ms + NORM_EPS) * g2_ref[...]) * (1.0 + ada_ref[0, 4:5, :]) + ada_ref[0, 3:4, :]
    h2_ref[...] = h2

    logits = jnp.dot(h2, wr_ref[...], preferred_element_type=f32, precision=lax.Precision.HIGHEST)
    lane = lax.broadcasted_iota(jnp.int32, (tm, LANES), 1)
    big = jnp.int32(LANES)
    rmax = lambda a: jnp.max(a, axis=-1, keepdims=True)
    rmin = lambda a: jnp.min(a, axis=-1, keepdims=True)
    rsum = lambda a: jnp.sum(a, axis=-1, keepdims=True)
    is_grp = lane < N_EXPERT_GROUPS
    m_g = rmax(jnp.where(is_grp, logits, -jnp.inf))
    p_g = 1.0 / rsum(jnp.where(is_grp, jnp.exp(logits - m_g), 0.0))
    g_sel = rmin(jnp.where(is_grp & (logits == m_g), lane, big))
    lo = N_EXPERT_GROUPS + EXPERTS_PER_GROUP * g_sel
    in_grp = (lane >= lo) & (lane < lo + EXPERTS_PER_GROUP)
    le = jnp.where(in_grp, logits, -jnp.inf)
    m_1 = rmax(le)
    i_1 = rmin(jnp.where(le == m_1, lane, big))
    le2 = jnp.where(lane == i_1, -jnp.inf, le)
    m_2 = rmax(le2)
    i_2 = rmin(jnp.where(le2 == m_2, lane, big))
    e_2 = jnp.exp(m_2 - m_1)
    w_1 = p_g / (1.0 + e_2)
    w_2 = p_g * e_2 / (1.0 + e_2)
    eid_1 = i_1 - N_EXPERT_GROUPS
    eid_2 = i_2 - N_EXPERT_GROUPS

    oh1 = lane == eid_1
    oh2 = lane == eid_2
    onehot = jnp.where(oh1, 1.0, 0.0) + jnp.where(oh2, 1.0, 0.0)
    r_i = lax.broadcasted_iota(jnp.int32, (tm, tm), 0)
    c_i = lax.broadcasted_iota(jnp.int32, (tm, tm), 1)
    earlier = jnp.where(c_i < r_i, 1.0, 0.0).astype(bf)
    before = jnp.dot(earlier, onehot.astype(bf), preferred_element_type=f32) + run_ref[...]
    rank_1 = rsum(jnp.where(oh1, before, 0.0))
    rank_2 = rsum(jnp.where(oh2, before, 0.0))
    run_ref[...] = run_ref[...] + jnp.sum(onehot, axis=0, keepdims=True)
    counts_ref[...] = run_ref[...]

    meta = jnp.zeros((tm, LANES), f32)
    for off, val in ((_META_EID, eid_1.astype(f32)), (_META_EID + 1, eid_2.astype(f32)),
                     (_META_W, w_1), (_META_W + 1, w_2), (_META_RANK, rank_1), (_META_RANK + 1, rank_2)):
        meta = jnp.where(lane == off, val, meta)
    meta_ref[...] = meta


def _mix(attn, p, x2d, ada3, w_pool, pool_scale, w_out, norm2_g, w_grp, w_exp, B, S):
    T, D = x2d.shape
    tm = min(256, S)
    tiles_per_seq = S // tm
    halo_blocks = tm // POOL_HALO
    bf = jnp.bfloat16
    w_router = jnp.concatenate(
        [w_grp, w_exp, jnp.zeros((D, LANES - N_EXPERT_GROUPS - N_EXPERTS), w_grp.dtype)], axis=1)
    row = lambda w: pl.BlockSpec((tm, w), lambda i: (i, 0))
    const = lambda shape: pl.BlockSpec(shape, lambda i: (0,) * len(shape))
    once = lambda shape: pl.BlockSpec(shape, lambda i: (0,) * len(shape), pipeline_mode=pl.Buffered(1))
    kern = functools.partial(_mix_kernel, tm=tm, tiles_per_seq=tiles_per_seq)
    return pl.pallas_call(
        kern,
        grid=(T // tm,),
        in_specs=[row(ATTN_WIDTH), row(POOL_WIDTH),
                  pl.BlockSpec((POOL_HALO, POOL_WIDTH), lambda i: (jnp.maximum(i * halo_blocks - 1, 0), 0)),
                  row(D),
                  pl.BlockSpec((1, 6, D), lambda i: (i // tiles_per_seq, 0, 0)),
                  once((POOL_GROUPS, POOL_GROUP_DIM, POOL_GROUP_DIM)), const((1, POOL_WIDTH)),
                  once((ATTN_WIDTH + POOL_WIDTH, D)), const((1, D)), once((D, LANES))],
        out_specs=[row(D), row(D), row(LANES), const((1, LANES))],
        out_shape=[jax.ShapeDtypeStruct((T, D), jnp.float32),
                   jax.ShapeDtypeStruct((T, D), jnp.float32),
                   jax.ShapeDtypeStruct((T, LANES), jnp.float32),
                   jax.ShapeDtypeStruct((1, LANES), jnp.float32)],
        scratch_shapes=[pltpu.VMEM((POOL_HALO + tm, POOL_WIDTH), jnp.float32),
                        pltpu.VMEM((1, LANES), jnp.float32)],
        compiler_params=_cparams(("arbitrary",)),
        name="mix",
    )(attn, p, p, x2d, ada3, w_pool.astype(bf), pool_scale.reshape(1, POOL_WIDTH), w_out.astype(bf),
      norm2_g.reshape(1, D), w_router)


def _dispatch_kernel(pad_start_ref, pad_len_ref, pos_ref, h2_ref, xs_ref, zeros_ref, sem, *, rt):
    i = pl.program_id(0)

    @pl.when(i == 0)
    def _():
        zeros_ref[...] = jnp.zeros(zeros_ref.shape, zeros_ref.dtype)

        def zero_row(r):
            return pltpu.make_async_copy(zeros_ref, xs_ref.at[pl.ds(r, 1)], sem)

        def fill(e, carry):
            start = pad_start_ref[e]
            n = pad_len_ref[e]
            lax.fori_loop(0, n, lambda r, c: (zero_row(start + r).start(), c)[1], 0)
            lax.fori_loop(0, n, lambda r, c: (zero_row(0).wait(), c)[1], 0)
            return carry

        lax.fori_loop(0, N_EXPERTS + 1, fill, 0)

    def row_copy(t, slot):
        return pltpu.make_async_copy(h2_ref.at[pl.ds(t, 1)], xs_ref.at[pl.ds(pos_ref[0, 0, 2 * t + slot], 1)], sem)

    def issue(t, carry):
        row_copy(t, 0).start()
        row_copy(t, 1).start()
        return carry

    def drain(t, carry):
        for _ in range(2):
            pltpu.make_async_copy(h2_ref.at[pl.ds(0, 1)], xs_ref.at[pl.ds(0, 1)], sem).wait()
        return carry

    lax.fori_loop(0, rt, issue, 0)
    lax.fori_loop(0, rt, drain, 0)


def _dispatch(h2, pos, pad_start, pad_len, n_rows):
    T, D = h2.shape
    rt = min(ROW_TILE, T)
    kern = functools.partial(_dispatch_kernel, rt=rt)
    return pl.pallas_call(
        kern,
        grid_spec=pltpu.PrefetchScalarGridSpec(
            num_scalar_prefetch=2,
            grid=(T // rt,),
            in_specs=[pl.BlockSpec((1, 1, 2 * rt), lambda i, *_: (i, 0, 0), memory_space=pltpu.SMEM),
                      pl.BlockSpec((rt, D), lambda i, *_: (i, 0))],
            out_specs=pl.BlockSpec(memory_space=pl.ANY),
            scratch_shapes=[pltpu.VMEM((1, D), h2.dtype), pltpu.SemaphoreType.DMA(())]),
        out_shape=jax.ShapeDtypeStruct((n_rows, D), h2.dtype),
        compiler_params=_cparams(("arbitrary",)),
        name="dispatch",
    )(pad_start, pad_len, pos.reshape(T // rt, 1, 2 * rt), h2)


def _experts_kernel(tile_expert_ref, n_used_ref, xs_ref, w1_ref, w3_ref, w2_ref, ys_ref):
    i = pl.program_id(0)

    @pl.when(i < n_used_ref[0])
    def _():
        xb = xs_ref[...].astype(jnp.bfloat16)
        a1 = jnp.dot(xb, w1_ref[0], preferred_element_type=jnp.float32)
        a3 = jnp.dot(xb, w3_ref[0], preferred_element_type=jnp.float32)
        act = (a1 * (1.0 / (1.0 + jnp.exp(-a1))) * a3).astype(jnp.bfloat16)
        ys_ref[...] = jnp.dot(act, w2_ref[0], preferred_element_type=jnp.float32)

    @pl.when(i >= n_used_ref[0])
    def _():
        ys_ref[...] = jnp.zeros(ys_ref.shape, ys_ref.dtype)


def _experts(xs, w1, w3, w2, tile_expert, n_used):
    n_rows, D = xs.shape
    n_tiles = n_rows // EXPERT_TILE
    bf = jnp.bfloat16
    return pl.pallas_call(
        _experts_kernel,
        grid_spec=pltpu.PrefetchScalarGridSpec(
            num_scalar_prefetch=2,
            grid=(n_tiles,),
            in_specs=[pl.BlockSpec((EXPERT_TILE, D), lambda i, te, nu: (jnp.minimum(i, nu[0] - 1), 0)),
                      pl.BlockSpec((1, D, D_EXPERT), lambda i, te, nu: (te[i], 0, 0)),
                      pl.BlockSpec((1, D, D_EXPERT), lambda i, te, nu: (te[i], 0, 0)),
                      pl.BlockSpec((1, D_EXPERT, D), lambda i, te, nu: (te[i], 0, 0))],
            out_specs=pl.BlockSpec((EXPERT_TILE, D), lambda i, te, nu: (i, 0))),
        out_shape=jax.ShapeDtypeStruct((n_rows, D), jnp.float32),
        compiler_params=_cparams(("arbitrary",)),
        name="experts",
    )(tile_expert, n_used, xs, w1.astype(bf), w3.astype(bf), w2.astype(bf))


def _combine_kernel(pos_ref, ys_ref, x1_ref, meta_ref, ada_ref, o_ref, rows_ref, sem, *, rt):
    def row_copy(t, slot):
        return pltpu.make_async_copy(ys_ref.at[pl.ds(pos_ref[0, 0, 2 * t + slot], 1)],
                                     rows_ref.at[slot, pl.ds(t, 1)], sem)

    def issue(t, carry):
        row_copy(t, 0).start()
        row_copy(t, 1).start()
        return carry

    def drain(t, carry):
        for slot in range(2):
            pltpu.make_async_copy(ys_ref.at[pl.ds(0, 1)], rows_ref.at[slot, pl.ds(0, 1)], sem).wait()
        return carry

    lax.fori_loop(0, rt, issue, 0)
    lax.fori_loop(0, rt, drain, 0)
    meta = meta_ref[...]
    y = meta[:, _META_W:_META_W + 1] * rows_ref[0] + meta[:, _META_W + 1:_META_W + 2] * rows_ref[1]
    o_ref[...] = x1_ref[...] + ada_ref[0, 5:6, :] * y


def _combine(ys, pos, x1, meta, ada3, S):
    T, D = x1.shape
    rt = min(ROW_TILE, S)
    tiles_per_seq = S // rt
    kern = functools.partial(_combine_kernel, rt=rt)
    return pl.pallas_call(
        kern,
        grid=(T // rt,),
        in_specs=[pl.BlockSpec((1, 1, 2 * rt), lambda i: (i, 0, 0), memory_space=pltpu.SMEM),
                  pl.BlockSpec(memory_space=pl.ANY),
                  pl.BlockSpec((rt, D), lambda i: (i, 0)),
                  pl.BlockSpec((rt, LANES), lambda i: (i, 0)),
                  pl.BlockSpec((1, 6, D), lambda i: (i // tiles_per_seq, 0, 0))],
        out_specs=pl.BlockSpec((rt, D), lambda i: (i, 0)),
        out_shape=jax.ShapeDtypeStruct((T, D), jnp.float32),
        scratch_shapes=[pltpu.VMEM((2, rt, D), jnp.float32), pltpu.SemaphoreType.DMA(())],
        compiler_params=_cparams(("arbitrary",)),
        name="combine",
    )(pos.reshape(T // rt, 1, 2 * rt), ys, x1, meta, ada3)


def _routing_plan(meta, counts, T):
    i32 = jnp.int32
    eid = meta[:, _META_EID:_META_EID + 2].astype(i32)
    rank = meta[:, _META_RANK:_META_RANK + 2].astype(i32)
    cnt = counts[0, :N_EXPERTS].astype(i32)
    padded = (cnt + EXPERT_TILE - 1) // EXPERT_TILE * EXPERT_TILE
    ends = jnp.cumsum(padded)
    starts = ends - padded
    pos = starts[eid] + rank
    n_rows = 2 * T + N_EXPERTS * EXPERT_TILE
    n_tiles = n_rows // EXPERT_TILE
    n_used = (ends[-1] // EXPERT_TILE).astype(i32)
    tile_start = jnp.minimum(jnp.arange(n_tiles, dtype=i32), n_used - 1) * EXPERT_TILE
    tile_expert = jnp.minimum(jnp.searchsorted(ends, tile_start, side="right"), N_EXPERTS - 1).astype(i32)
    pad_start = jnp.concatenate([starts + cnt, ends[-1:]]).astype(i32)
    pad_len = jnp.concatenate([padded - cnt, n_rows - ends[-1:]]).astype(i32)
    return pos, pad_start, pad_len, tile_expert, n_used.reshape(1), n_rows


def kernel(x, c, w_ada, b_ada, norm1_g, w_in, q_norm_g, k_norm_g, w_pool, pool_scale,
           w_out, norm2_g, w_grp, w_exp, w1, w3, w2):
    B, S, D = x.shape
    T = B * S
    x2d = x.reshape(T, D)
    for l in range(w_ada.shape[0]):
        ada3 = _ada(c, w_ada[l], b_ada[l]).reshape(B, 6, D)
        q, kt, v, qi, kia, kib, wi, p = _inproj(x2d, ada3, norm1_g[l], w_in[l], q_norm_g[l], k_norm_g[l], B, S)
        attn = _dsa(q, qi, wi, kia, kib, kt, v, B, S)
        x1, h2, meta, counts = _mix(attn, p, x2d, ada3, w_pool[l], pool_scale[l], w_out[l], norm2_g[l],
                                    w_grp[l], w_exp[l], B, S)
        pos, pad_start, pad_len, tile_expert, n_used, n_rows = _routing_plan(meta, counts, T)
        xs = _dispatch(h2, pos, pad_start, pad_len, n_rows)
        ys = _experts(xs, w1[l], w3[l], w2[l], tile_expert, n_used)
        x2d = _combine(ys, pos, x1, meta, ada3, S)
    return x2d.reshape(B, S, D)
```

```python
import functools

import numpy as np
import jax
import jax.numpy as jnp
from jax import lax
from jax.experimental import pallas as pl
from jax.experimental.pallas import tpu as pltpu

N_HEADS = 8
N_KV_HEADS = 2
HEAD_DIM = 128
ATTN_WIDTH = N_HEADS * HEAD_DIM
KV_WIDTH = N_KV_HEADS * HEAD_DIM
ROPE_THETA = 500000.0
ROPE_FRACTION = 4
IDX_HEADS = 16
IDX_DIM = 64
TOPK_MAX = 256
POOL_GROUPS = 4
POOL_WINDOWS = (2, 4, 8, 16)
POOL_GROUP_DIM = 256
POOL_WIDTH = POOL_GROUPS * POOL_GROUP_DIM
N_EXPERT_GROUPS = 4
EXPERTS_PER_GROUP = 8
N_EXPERTS = N_EXPERT_GROUPS * EXPERTS_PER_GROUP
D_EXPERT = 512
NORM_EPS = 1e-6

LANES = 128
VMEM_LIMIT_BYTES = 56 * 1024 * 1024

KEY_TILE = 512
Q_TILE = 128
POOL_HALO = 16
EXPERT_TILE = 256
ROW_TILE = 256

_C_Q = 0
_C_K = _C_Q + ATTN_WIDTH
_C_V = _C_K + KV_WIDTH
_C_QI = _C_V + KV_WIDTH
_C_KI = _C_QI + IDX_HEADS * IDX_DIM
_C_WI = _C_KI + LANES
_C_P = _C_WI + LANES
_C_END = _C_P + POOL_WIDTH

_NEG = -1e30
_LOG2_E = 1.4426950408889634
_INT_MIN = -2 ** 31
_KEY_LOWEST_FINITE = -2 ** 31 + 0x00800000


def _cparams(semantics):
    return pltpu.CompilerParams(dimension_semantics=semantics, vmem_limit_bytes=VMEM_LIMIT_BYTES)


def _ada_kernel(c_ref, w_ref, b_ref, o_ref):
    c = c_ref[...]
    s = c * (1.0 / (1.0 + jnp.exp(-c)))
    o_ref[...] = jnp.dot(s, w_ref[...], preferred_element_type=jnp.float32,
                         precision=lax.Precision.HIGHEST) + b_ref[...]


def _ada(c, w_ada, b_ada):
    B, D = c.shape
    N = w_ada.shape[1]
    tn = 1024
    rows = 8
    c_pad = jnp.zeros((rows, D), jnp.float32).at[:B].set(c)
    out = pl.pallas_call(
        _ada_kernel,
        grid=(N // tn,),
        in_specs=[pl.BlockSpec((rows, D), lambda j: (0, 0)),
                  pl.BlockSpec((D, tn), lambda j: (0, j)),
                  pl.BlockSpec((1, tn), lambda j: (0, j))],
        out_specs=pl.BlockSpec((rows, tn), lambda j: (0, j)),
        out_shape=jax.ShapeDtypeStruct((rows, N), jnp.float32),
        compiler_params=_cparams(("arbitrary",)),
        name="ada",
    )(c_pad, w_ada, b_ada.reshape(1, N))
    return out[:B]


def _rope_tables(S, head_dim):
    rd = head_dim // ROPE_FRACTION
    half = rd // 2
    pos = jnp.arange(S, dtype=jnp.float32)
    inv = jnp.float32(ROPE_THETA) ** (-(jnp.arange(half, dtype=jnp.float32) * 2.0) / rd)
    ang = pos[:, None] * inv[None, :]
    cos, sin = jnp.cos(ang), jnp.sin(ang)
    lane = np.arange(LANES) % head_dim
    fidx = np.where(lane < half, lane, lane - half) % half
    in_lo = jnp.asarray(lane < half)[None, :]
    in_hi = jnp.asarray((lane >= half) & (lane < rd))[None, :]
    cos_l, sin_l = cos[:, fidx], sin[:, fidx]
    c_tab = jnp.where(in_lo | in_hi, cos_l, 1.0)
    s1_tab = jnp.where(in_lo, -sin_l, 0.0)
    s2_tab = jnp.where(in_hi, sin_l, 0.0)
    return c_tab, s1_tab, s2_tab, half


def _rope(x, c_tab, s1_tab, s2_tab, half):
    return (x * c_tab + pltpu.roll(x, LANES - half, 1) * s1_tab + pltpu.roll(x, half, 1) * s2_tab)


def _inproj_kernel(x_ref, ada_ref, g1_ref, w_ref, qg_ref, kg_ref,
                   cq_ref, s1q_ref, s2q_ref, ci_ref, s1i_ref, s2i_ref,
                   q_ref, kt_ref, v_ref, qi_ref, kia_ref, kib_ref, wi_ref, p_ref,
                   *, half_qk, half_idx, n_chunks):
    x = x_ref[...]
    ms = jnp.mean(x * x, axis=-1, keepdims=True)
    xn = x * lax.rsqrt(ms + NORM_EPS) * g1_ref[...]
    h = (xn * (1.0 + ada_ref[0, 1:2, :]) + ada_ref[0, 0:1, :]).astype(jnp.bfloat16)

    def proj(lo, hi):
        return jnp.dot(h, w_ref[:, lo:hi], preferred_element_type=jnp.float32)

    cq, s1q, s2q = cq_ref[...], s1q_ref[...], s2q_ref[...]
    ci, s1i, s2i = ci_ref[...], s1i_ref[...], s2i_ref[...]

    def qk_head(slab, gain):
        m = jnp.mean(slab * slab, axis=-1, keepdims=True)
        y = slab * lax.rsqrt(m + NORM_EPS) * gain
        return _rope(y, cq, s1q, s2q, half_qk)

    attn_scale = HEAD_DIM ** -0.5 * _LOG2_E
    q = proj(_C_Q, _C_K)
    for hd in range(N_HEADS):
        sl = slice(hd * HEAD_DIM, (hd + 1) * HEAD_DIM)
        q_ref[:, sl] = (qk_head(q[:, sl], qg_ref[...]) * attn_scale).astype(jnp.bfloat16)

    k = proj(_C_K, _C_V)
    for g in range(N_KV_HEADS):
        kt = qk_head(k[:, g * HEAD_DIM:(g + 1) * HEAD_DIM], kg_ref[...]).T.astype(jnp.bfloat16)
        for c in range(n_chunks):
            kt_ref[0, g, c] = kt[:, c * KEY_TILE:(c + 1) * KEY_TILE]

    v_ref[...] = proj(_C_V, _C_QI).astype(jnp.bfloat16)

    qi = proj(_C_QI, _C_KI)
    for j in range(IDX_HEADS * IDX_DIM // LANES):
        sl = slice(j * LANES, (j + 1) * LANES)
        qi_ref[:, sl] = _rope(qi[:, sl], ci, s1i, s2i, half_idx).astype(jnp.bfloat16)

    ki = _rope(proj(_C_KI, _C_WI), ci, s1i, s2i, half_idx).T
    ki_swapped = jnp.concatenate([ki[IDX_DIM:], ki[:IDX_DIM]], axis=0)
    for c in range(n_chunks):
        kia_ref[0, c] = ki[:, c * KEY_TILE:(c + 1) * KEY_TILE].astype(jnp.bfloat16)
        kib_ref[0, c] = ki_swapped[:, c * KEY_TILE:(c + 1) * KEY_TILE].astype(jnp.bfloat16)

    idx_scale = (IDX_DIM ** -0.5) * (IDX_HEADS ** -0.5)
    wi_ref[...] = proj(_C_WI, _C_P) * idx_scale
    p_ref[...] = proj(_C_P, _C_END)


def _inproj(x2d, ada3, norm1_g, w_in, q_norm_g, k_norm_g, B, S):
    T, D = x2d.shape
    tm = min(512, S)
    n_chunks = tm // KEY_TILE
    tiles_per_seq = S // tm
    bf = jnp.bfloat16

    offs = np.cumsum((ATTN_WIDTH, KV_WIDTH, KV_WIDTH, IDX_HEADS * IDX_DIM, IDX_DIM, IDX_HEADS))
    o_q, o_k, o_v, o_qi, o_ki, o_wi = [int(o) for o in offs]
    zeros = lambda n: jnp.zeros((D, n), w_in.dtype)
    w_pad = jnp.concatenate([
        w_in[:, :o_qi],
        w_in[:, o_qi:o_ki], zeros(LANES - IDX_DIM),
        w_in[:, o_ki:o_wi], zeros(LANES - IDX_HEADS),
        w_in[:, o_wi:],
    ], axis=1).astype(bf)
    assert w_pad.shape[1] == _C_END

    cq, s1q, s2q, half_qk = _rope_tables(S, HEAD_DIM)
    ci, s1i, s2i, half_idx = _rope_tables(S, IDX_DIM)

    row = lambda w: pl.BlockSpec((tm, w), lambda i: (i, 0))
    tab = pl.BlockSpec((tm, LANES), lambda i: (i % tiles_per_seq, 0))
    const = lambda shape: pl.BlockSpec(shape, lambda i: (0,) * len(shape))
    kern = functools.partial(_inproj_kernel, half_qk=half_qk, half_idx=half_idx, n_chunks=n_chunks)
    return pl.pallas_call(
        kern,
        grid=(T // tm,),
        in_specs=[row(D),
                  pl.BlockSpec((1, 6, D), lambda i: (i // tiles_per_seq, 0, 0)),
                  const((1, D)),
                  pl.BlockSpec((D, _C_END), lambda i: (0, 0), pipeline_mode=pl.Buffered(1)),
                  const((1, HEAD_DIM)), const((1, HEAD_DIM)),
                  tab, tab, tab, tab, tab, tab],
        out_specs=[row(ATTN_WIDTH),
                   pl.BlockSpec((1, N_KV_HEADS, n_chunks, HEAD_DIM, KEY_TILE),
                                lambda i: (i // tiles_per_seq, 0, i % tiles_per_seq, 0, 0)),
                   row(KV_WIDTH),
                   row(IDX_HEADS * IDX_DIM),
                   pl.BlockSpec((1, n_chunks, LANES, KEY_TILE),
                                lambda i: (i // tiles_per_seq, i % tiles_per_seq, 0, 0)),
                   pl.BlockSpec((1, n_chunks, LANES, KEY_TILE),
                                lambda i: (i // tiles_per_seq, i % tiles_per_seq, 0, 0)),
                   row(LANES),
                   row(POOL_WIDTH)],
        out_shape=[jax.ShapeDtypeStruct((T, ATTN_WIDTH), bf),
                   jax.ShapeDtypeStruct((B, N_KV_HEADS, S // KEY_TILE, HEAD_DIM, KEY_TILE), bf),
                   jax.ShapeDtypeStruct((T, KV_WIDTH), bf),
                   jax.ShapeDtypeStruct((T, IDX_HEADS * IDX_DIM), bf),
                   jax.ShapeDtypeStruct((B, S // KEY_TILE, LANES, KEY_TILE), bf),
                   jax.ShapeDtypeStruct((B, S // KEY_TILE, LANES, KEY_TILE), bf),
                   jax.ShapeDtypeStruct((T, LANES), jnp.float32),
                   jax.ShapeDtypeStruct((T, POOL_WIDTH), jnp.float32)],
        compiler_params=_cparams(("arbitrary",)),
        name="inproj",
    )(x2d, ada3, norm1_g.reshape(1, D), w_pad, q_norm_g.reshape(1, HEAD_DIM), k_norm_g.reshape(1, HEAD_DIM),
      cq, s1q, s2q, ci, s1i, s2i)


def _dsa_kernel(q_ref, qi_ref, wi_ref, kia_ref, kib_ref, kt_ref, v_ref, o_ref,
                keys_ref, keyst_ref, wb_ref, m_ref, acc_ref, s_ref, *, tq, n_sel, idx_bits):
    qb = pl.program_id(1)
    n_tiles = (qb * tq + tq + KEY_TILE - 1) // KEY_TILE
    lane_chunks = KEY_TILE // LANES
    rep = N_HEADS // N_KV_HEADS
    f32 = jnp.float32

    qi = qi_ref[...]
    n_slabs = IDX_HEADS * IDX_DIM // LANES
    lhs = jnp.concatenate([qi[:, j * LANES:(j + 1) * LANES] for j in range(n_slabs)], axis=0)
    w = wi_ref[...]
    for hd in range(IDX_HEADS):
        wb_ref[hd] = jnp.broadcast_to(w[:, hd:hd + 1], (tq, LANES))
    t_pos = qb * tq + lax.broadcasted_iota(jnp.int32, (tq, KEY_TILE), 0)
    k_off = lax.broadcasted_iota(jnp.int32, (tq, KEY_TILE), 1)

    def score_tile(j, carry):
        za = jnp.dot(lhs, kia_ref[0, j], preferred_element_type=f32)
        zb = jnp.dot(lhs, kib_ref[0, j], preferred_element_type=f32)
        acc = jnp.zeros((tq, KEY_TILE), f32)
        for s in range(n_slabs):
            wa = jnp.concatenate([wb_ref[2 * s]] * lane_chunks, axis=1)
            wo = jnp.concatenate([wb_ref[2 * s + 1]] * lane_chunks, axis=1)
            acc = acc + wa * jnp.maximum(za[s * tq:(s + 1) * tq], 0.0)
            acc = acc + wo * jnp.maximum(zb[s * tq:(s + 1) * tq], 0.0)
        score = jnp.where(j * KEY_TILE + k_off <= t_pos, acc, -jnp.inf)
        bits = lax.bitcast_convert_type(score, jnp.int32)
        key = bits ^ ((bits >> 31) & 0x7FFFFFFF)
        keys_ref[j] = key
        for cc in range(lane_chunks):
            keyst_ref[j, cc * LANES:(cc + 1) * LANES, :] = key[:, cc * LANES:(cc + 1) * LANES].T
        return carry

    lax.fori_loop(0, n_tiles, score_tile, 0)

    def sweep_tiles(body, init=0):
        pairs = lax.shift_right_logical(n_tiles, 1)
        carry = lax.fori_loop(0, pairs, lambda i, c: body(2 * i + 1, body(2 * i, c)), init)
        return lax.fori_loop(2 * pairs, n_tiles, body, carry)

    def rows_to_lanes_sum(hit):
        return jnp.sum(hit.reshape(KEY_TILE // 8, 8, tq), axis=0)

    def count_ge(cand):
        def body(j, cnt):
            return cnt + rows_to_lanes_sum(jnp.where(keyst_ref[j] >= cand, 1, 0))
        cnt = sweep_tiles(body, jnp.zeros((8, tq), jnp.int32))
        return jnp.sum(cnt, axis=0, keepdims=True)

    def bisect(i, thr):
        cand = thr + lax.shift_left(jnp.int32(1), 31 - i)
        return jnp.where(count_ge(cand) >= n_sel, cand, thr)

    thr_t = lax.fori_loop(0, 32, bisect, jnp.full((1, tq), _INT_MIN, jnp.int32))

    def per_query_rows(vec_t):
        rows = jnp.broadcast_to(vec_t, (tq, tq)).T
        return jnp.concatenate([rows] * lane_chunks, axis=1)

    thr_wide = per_query_rows(thr_t)

    tied = (count_ge(thr_t) > n_sel) & (thr_t >= _KEY_LOWEST_FINITE)
    any_tied = jnp.max(jnp.where(tied, 1, 0)) > 0

    @pl.when(any_tied)
    def _():
        need = n_sel - count_ge(thr_t + 1)
        row_idx = lax.broadcasted_iota(jnp.int32, (KEY_TILE, tq), 0)

        def tied_before(limit):
            def body(j, cnt):
                hit = jnp.where(j * KEY_TILE + row_idx < limit, 1, 0)
                return cnt + rows_to_lanes_sum(jnp.where(keyst_ref[j] == thr_t, hit, 0))
            cnt = lax.fori_loop(0, n_tiles, body, jnp.zeros((8, tq), jnp.int32))
            return jnp.sum(cnt, axis=0, keepdims=True)

        def bisect_index(i, last):
            cand = last + lax.shift_left(jnp.int32(1), idx_bits - 1 - i)
            return jnp.where(tied_before(cand) < need, cand, last)

        last_t = lax.fori_loop(0, idx_bits, bisect_index, jnp.zeros((1, tq), jnp.int32))
        last_wide = per_query_rows(last_t)

        def demote(j, carry):
            kk = keys_ref[j]
            drop = jnp.where(j * KEY_TILE + k_off > last_wide, thr_wide - 1, kk)
            keys_ref[j] = jnp.where(kk == thr_wide, drop, kk)
            return carry

        lax.fori_loop(0, n_tiles, demote, 0)

    thr_wide = jnp.maximum(thr_wide, _KEY_LOWEST_FINITE)

    q = q_ref[...]
    ones_col = jnp.where(lax.broadcasted_iota(jnp.int32, (KEY_TILE, HEAD_DIM), 1) == 0, 1.0, 0.0
                         ).astype(jnp.bfloat16)
    for g in range(N_KV_HEADS):
        qg = jnp.concatenate([q[:, (g * rep + r) * HEAD_DIM:(g * rep + r + 1) * HEAD_DIM] for r in range(rep)],
                             axis=0)
        m_ref[...] = jnp.full(m_ref.shape, _NEG, f32)
        acc_ref[...] = jnp.zeros(acc_ref.shape, f32)

        def scores(j, carry, qg=qg, g=g):
            bias = jnp.where(keys_ref[j] >= thr_wide, 0.0, _NEG)
            s = jnp.dot(qg, kt_ref[0, g, j], preferred_element_type=f32) + jnp.concatenate([bias] * rep, axis=0)
            s_ref[j] = s
            m_part = m_ref[...]
            for cc in range(lane_chunks):
                m_part = jnp.maximum(m_part, s[:, cc * LANES:(cc + 1) * LANES])
            m_ref[...] = m_part
            return carry

        sweep_tiles(scores)
        m_ref[...] = jnp.broadcast_to(jnp.max(m_ref[...], axis=-1, keepdims=True), m_ref.shape)

        def values(j, carry, g=g):
            m_row = m_ref[...]
            p = jnp.exp2(s_ref[j] - jnp.concatenate([m_row] * lane_chunks, axis=1)).astype(jnp.bfloat16)
            start = pl.multiple_of(j * KEY_TILE, KEY_TILE)
            vt = jnp.concatenate([v_ref[pl.ds(start, KEY_TILE), g * HEAD_DIM:(g + 1) * HEAD_DIM], ones_col],
                                 axis=1)
            acc_ref[...] += jnp.dot(p, vt, preferred_element_type=f32)
            return carry

        sweep_tiles(values)
        acc = acc_ref[...]
        o = acc[:, :HEAD_DIM] / acc[:, HEAD_DIM:HEAD_DIM + 1]
        for r in range(rep):
            hd = g * rep + r
            o_ref[:, hd * HEAD_DIM:(hd + 1) * HEAD_DIM] = o[r * tq:(r + 1) * tq].astype(o_ref.dtype)


def _dsa(q, qi, wi, kia, kib, kt, v, B, S):
    tq = min(Q_TILE, S)
    n_sel = min(TOPK_MAX, S // 4)
    nq = S // tq
    n_kt = S // KEY_TILE
    rep = N_HEADS // N_KV_HEADS
    row = lambda w: pl.BlockSpec((tq, w), lambda b, i: (b * nq + i, 0))
    once = pl.Buffered(1)
    kern = functools.partial(_dsa_kernel, tq=tq, n_sel=n_sel, idx_bits=max(1, (S - 1).bit_length()))
    return pl.pallas_call(
        kern,
        grid=(B, nq),
        in_specs=[row(ATTN_WIDTH), row(IDX_HEADS * IDX_DIM), row(LANES),
                  pl.BlockSpec((1, n_kt, LANES, KEY_TILE), lambda b, i: (b, 0, 0, 0), pipeline_mode=once),
                  pl.BlockSpec((1, n_kt, LANES, KEY_TILE), lambda b, i: (b, 0, 0, 0), pipeline_mode=once),
                  pl.BlockSpec((1, N_KV_HEADS, n_kt, HEAD_DIM, KEY_TILE), lambda b, i: (b, 0, 0, 0, 0),
                               pipeline_mode=once),
                  pl.BlockSpec((S, KV_WIDTH), lambda b, i: (b, 0), pipeline_mode=once)],
        out_specs=row(ATTN_WIDTH),
        out_shape=jax.ShapeDtypeStruct((B * S, ATTN_WIDTH), jnp.bfloat16),
        scratch_shapes=[pltpu.VMEM((n_kt, tq, KEY_TILE), jnp.int32),
                        pltpu.VMEM((n_kt, KEY_TILE, tq), jnp.int32),
                        pltpu.VMEM((IDX_HEADS, tq, LANES), jnp.float32),
                        pltpu.VMEM((rep * tq, LANES), jnp.float32),
                        pltpu.VMEM((rep * tq, 2 * HEAD_DIM), jnp.float32),
                        pltpu.VMEM((n_kt, rep * tq, KEY_TILE), jnp.float32)],
        compiler_params=_cparams(("arbitrary", "arbitrary")),
        name="dsa",
    )(q, qi, wi, kia, kib, kt, v)


_META_EID, _META_W, _META_RANK = 0, 2, 4


def _mix_kernel(attn_ref, p_ref, halo_ref, x_ref, ada_ref, wpool_ref, pscale_ref, wout_ref, g2_ref, wr_ref,
                x1_ref, h2_ref, meta_ref, counts_ref, pbuf_ref, run_ref, *, tm, tiles_per_seq):
    i = pl.program_id(0)
    f32 = jnp.float32
    bf = jnp.bfloat16
    seq_tile = i % tiles_per_seq

    @pl.when(i == 0)
    def _():
        run_ref[...] = jnp.zeros(run_ref.shape, f32)

    p = p_ref[...]
    pbuf_ref[0:POOL_HALO, :] = jnp.where(seq_tile == 0, 0.0, halo_ref[...])
    pbuf_ref[POOL_HALO:POOL_HALO + tm, :] = p
    t_pos = seq_tile * tm + lax.broadcasted_iota(jnp.int32, (tm, 1), 0)
    pieces = [attn_ref[...]]
    for g, win in enumerate(POOL_WINDOWS):
        cs = slice(g * POOL_GROUP_DIM, (g + 1) * POOL_GROUP_DIM)
        tot = p[:, cs]
        for back in range(1, win):
            tot = tot + pbuf_ref[POOL_HALO - back:POOL_HALO - back + tm, cs]
        count = jnp.minimum(t_pos + 1, win).astype(f32)
        mixed = tot / count - p[:, cs]
        y = jnp.dot(mixed.astype(bf), wpool_ref[g], preferred_element_type=f32) * pscale_ref[:, cs]
        pieces.append(y.astype(bf))
    mix = jnp.dot(jnp.concatenate(pieces, axis=1), wout_ref[...], preferred_element_type=f32)
    x1 = x_ref[...] + ada_ref[0, 2:3, :] * mix
    x1_ref[...] = x1

    ms = jnp.mean(x1 * x1, axis=-1, keepdims=True)
    h2 = (x1 * lax.rsqrt(ms + NORM_EPS) * g2_ref[...]) * (1.0 + ada_ref[0, 4:5, :]) + ada_ref[0, 3:4, :]
    h2_ref[...] = h2

    logits = jnp.dot(h2, wr_ref[...], preferred_element_type=f32, precision=lax.Precision.HIGHEST)
    lane = lax.broadcasted_iota(jnp.int32, (tm, LANES), 1)
    big = jnp.int32(LANES)
    rmax = lambda a: jnp.max(a, axis=-1, keepdims=True)
    rmin = lambda a: jnp.min(a, axis=-1, keepdims=True)
    rsum = lambda a: jnp.sum(a, axis=-1, keepdims=True)
    is_grp = lane < N_EXPERT_GROUPS
    m_g = rmax(jnp.where(is_grp, logits, -jnp.inf))
    p_g = 1.0 / rsum(jnp.where(is_grp, jnp.exp(logits - m_g), 0.0))
    g_sel = rmin(jnp.where(is_grp & (logits == m_g), lane, big))
    lo = N_EXPERT_GROUPS + EXPERTS_PER_GROUP * g_sel
    in_grp = (lane >= lo) & (lane < lo + EXPERTS_PER_GROUP)
    le = jnp.where(in_grp, logits, -jnp.inf)
    m_1 = rmax(le)
    i_1 = rmin(jnp.where(le == m_1, lane, big))
    le2 = jnp.where(lane == i_1, -jnp.inf, le)
    m_2 = rmax(le2)
    i_2 = rmin(jnp.where(le2 == m_2, lane, big))
    e_2 = jnp.exp(m_2 - m_1)
    w_1 = p_g / (1.0 + e_2)
    w_2 = p_g * e_2 / (1.0 + e_2)
    eid_1 = i_1 - N_EXPERT_GROUPS
    eid_2 = i_2 - N_EXPERT_GROUPS

    oh1 = lane == eid_1
    oh2 = lane == eid_2
    onehot = jnp.where(oh1, 1.0, 0.0) + jnp.where(oh2, 1.0, 0.0)
    r_i = lax.broadcasted_iota(jnp.int32, (tm, tm), 0)
    c_i = lax.broadcasted_iota(jnp.int32, (tm, tm), 1)
    earlier = jnp.where(c_i < r_i, 1.0, 0.0).astype(bf)
    before = jnp.dot(earlier, onehot.astype(bf), preferred_element_type=f32) + run_ref[...]
    rank_1 = rsum(jnp.where(oh1, before, 0.0))
    rank_2 = rsum(jnp.where(oh2, before, 0.0))
    run_ref[...] = run_ref[...] + jnp.sum(onehot, axis=0, keepdims=True)
    counts_ref[...] = run_ref[...]

    meta = jnp.zeros((tm, LANES), f32)
    for off, val in ((_META_EID, eid_1.astype(f32)), (_META_EID + 1, eid_2.astype(f32)),
                     (_META_W, w_1), (_META_W + 1, w_2), (_META_RANK, rank_1), (_META_RANK + 1, rank_2)):
        meta = jnp.where(lane == off, val, meta)
    meta_ref[...] = meta


def _mix(attn, p, x2d, ada3, w_pool, pool_scale, w_out, norm2_g, w_grp, w_exp, B, S):
    T, D = x2d.shape
    tm = min(256, S)
    tiles_per_seq = S // tm
    halo_blocks = tm // POOL_HALO
    bf = jnp.bfloat16
    w_router = jnp.concatenate(
        [w_grp, w_exp, jnp.zeros((D, LANES - N_EXPERT_GROUPS - N_EXPERTS), w_grp.dtype)], axis=1)
    row = lambda w: pl.BlockSpec((tm, w), lambda i: (i, 0))
    const = lambda shape: pl.BlockSpec(shape, lambda i: (0,) * len(shape))
    once = lambda shape: pl.BlockSpec(shape, lambda i: (0,) * len(shape), pipeline_mode=pl.Buffered(1))
    kern = functools.partial(_mix_kernel, tm=tm, tiles_per_seq=tiles_per_seq)
    return pl.pallas_call(
        kern,
        grid=(T // tm,),
        in_specs=[row(ATTN_WIDTH), row(POOL_WIDTH),
                  pl.BlockSpec((POOL_HALO, POOL_WIDTH), lambda i: (jnp.maximum(i * halo_blocks - 1, 0), 0)),
                  row(D),
                  pl.BlockSpec((1, 6, D), lambda i: (i // tiles_per_seq, 0, 0)),
                  once((POOL_GROUPS, POOL_GROUP_DIM, POOL_GROUP_DIM)), const((1, POOL_WIDTH)),
                  once((ATTN_WIDTH + POOL_WIDTH, D)), const((1, D)), once((D, LANES))],
        out_specs=[row(D), row(D), row(LANES), const((1, LANES))],
        out_shape=[jax.ShapeDtypeStruct((T, D), jnp.float32),
                   jax.ShapeDtypeStruct((T, D), jnp.float32),
                   jax.ShapeDtypeStruct((T, LANES), jnp.float32),
                   jax.ShapeDtypeStruct((1, LANES), jnp.float32)],
        scratch_shapes=[pltpu.VMEM((POOL_HALO + tm, POOL_WIDTH), jnp.float32),
                        pltpu.VMEM((1, LANES), jnp.float32)],
        compiler_params=_cparams(("arbitrary",)),
        name="mix",
    )(attn, p, p, x2d, ada3, w_pool.astype(bf), pool_scale.reshape(1, POOL_WIDTH), w_out.astype(bf),
      norm2_g.reshape(1, D), w_router)


def _dispatch_kernel(pad_start_ref, pad_len_ref, n_used_ref, pos_ref, h2_ref, xs_ref, zeros_ref, sem, *, rt, n_tiles):
    i = pl.program_id(0)

    @pl.when(i == 0)
    def _():
        zeros_ref[...] = jnp.zeros(zeros_ref.shape, zeros_ref.dtype)

        def zero_row(r):
            return pltpu.make_async_copy(zeros_ref.at[pl.ds(0, 1)], xs_ref.at[pl.ds(r, 1)], sem)

        def zero_tile(t):
            start = pl.multiple_of(t * EXPERT_TILE, EXPERT_TILE)
            return pltpu.make_async_copy(zeros_ref, xs_ref.at[pl.ds(start, EXPERT_TILE)], sem)

        def fill(e, carry):
            start = pad_start_ref[e]
            n = pad_len_ref[e]
            lax.fori_loop(0, n, lambda r, c: (zero_row(start + r).start(), c)[1], 0)
            lax.fori_loop(0, n, lambda r, c: (zero_row(0).wait(), c)[1], 0)
            return carry

        lax.fori_loop(0, N_EXPERTS, fill, 0)
        first_unused = n_used_ref[0]
        lax.fori_loop(first_unused, n_tiles, lambda t, c: (zero_tile(t).start(), c)[1], 0)
        lax.fori_loop(first_unused, n_tiles, lambda t, c: (zero_tile(0).wait(), c)[1], 0)

    def row_copy(t, slot):
        return pltpu.make_async_copy(h2_ref.at[pl.ds(t, 1)], xs_ref.at[pl.ds(pos_ref[0, 0, 2 * t + slot], 1)], sem)

    def issue(t, carry):
        row_copy(t, 0).start(priority=0)
        row_copy(t, 1).start(priority=1)
        return carry

    def drain(t, carry):
        for _ in range(2):
            pltpu.make_async_copy(h2_ref.at[pl.ds(0, 1)], xs_ref.at[pl.ds(0, 1)], sem).wait()
        return carry

    lax.fori_loop(0, rt, issue, 0)
    lax.fori_loop(0, rt, drain, 0)


def _dispatch(h2, pos, pad_start, pad_len, n_used, n_rows):
    T, D = h2.shape
    rt = min(ROW_TILE, T)
    kern = functools.partial(_dispatch_kernel, rt=rt, n_tiles=n_rows // EXPERT_TILE)
    return pl.pallas_call(
        kern,
        grid_spec=pltpu.PrefetchScalarGridSpec(
            num_scalar_prefetch=3,
            grid=(T // rt,),
            in_specs=[pl.BlockSpec((1, 1, 2 * rt), lambda i, *_: (i, 0, 0), memory_space=pltpu.SMEM),
                      pl.BlockSpec((rt, D), lambda i, *_: (i, 0))],
            out_specs=pl.BlockSpec(memory_space=pl.ANY),
            scratch_shapes=[pltpu.VMEM((EXPERT_TILE, D), h2.dtype), pltpu.SemaphoreType.DMA(())]),
        out_shape=jax.ShapeDtypeStruct((n_rows, D), h2.dtype),
        compiler_params=_cparams(("arbitrary",)),
        name="dispatch",
    )(pad_start, pad_len, n_used, pos.reshape(T // rt, 1, 2 * rt), h2)


def _experts_kernel(tile_expert_ref, n_used_ref, xs_ref, w1_ref, w3_ref, w2_ref, ys_ref):
    i = pl.program_id(0)

    @pl.when(i < n_used_ref[0])
    def _():
        xb = xs_ref[...].astype(jnp.bfloat16)
        a1 = jnp.dot(xb, w1_ref[0], preferred_element_type=jnp.float32)
        a3 = jnp.dot(xb, w3_ref[0], preferred_element_type=jnp.float32)
        act = (a1 * (1.0 / (1.0 + jnp.exp(-a1))) * a3).astype(jnp.bfloat16)
        ys_ref[...] = jnp.dot(act, w2_ref[0], preferred_element_type=jnp.float32)

    @pl.when(i >= n_used_ref[0])
    def _():
        ys_ref[...] = jnp.zeros(ys_ref.shape, ys_ref.dtype)


def _experts(xs, w1, w3, w2, tile_expert, n_used):
    n_rows, D = xs.shape
    n_tiles = n_rows // EXPERT_TILE
    bf = jnp.bfloat16
    return pl.pallas_call(
        _experts_kernel,
        grid_spec=pltpu.PrefetchScalarGridSpec(
            num_scalar_prefetch=2,
            grid=(n_tiles,),
            in_specs=[pl.BlockSpec((EXPERT_TILE, D), lambda i, te, nu: (jnp.minimum(i, nu[0] - 1), 0)),
                      pl.BlockSpec((1, D, D_EXPERT), lambda i, te, nu: (te[i], 0, 0)),
                      pl.BlockSpec((1, D, D_EXPERT), lambda i, te, nu: (te[i], 0, 0)),
                      pl.BlockSpec((1, D_EXPERT, D), lambda i, te, nu: (te[i], 0, 0))],
            out_specs=pl.BlockSpec((EXPERT_TILE, D), lambda i, te, nu: (i, 0))),
        out_shape=jax.ShapeDtypeStruct((n_rows, D), jnp.float32),
        compiler_params=_cparams(("arbitrary",)),
        name="experts",
    )(tile_expert, n_used, xs, w1.astype(bf), w3.astype(bf), w2.astype(bf))


def _combine_kernel(pos_ref, ys_ref, x1_ref, meta_ref, ada_ref, o_ref, rows_ref, sem, *, rt):
    def row_copy(t, slot):
        return pltpu.make_async_copy(ys_ref.at[pl.ds(pos_ref[0, 0, 2 * t + slot], 1)],
                                     rows_ref.at[slot, pl.ds(t, 1)], sem)

    def issue(t, carry):
        row_copy(t, 0).start(priority=0)
        row_copy(t, 1).start(priority=1)
        return carry

    def drain(t, carry):
        for slot in range(2):
            pltpu.make_async_copy(ys_ref.at[pl.ds(0, 1)], rows_ref.at[slot, pl.ds(0, 1)], sem).wait()
        return carry

    lax.fori_loop(0, rt, issue, 0)
    lax.fori_loop(0, rt, drain, 0)
    meta = meta_ref[...]
    y = meta[:, _META_W:_META_W + 1] * rows_ref[0] + meta[:, _META_W + 1:_META_W + 2] * rows_ref[1]
    o_ref[...] = x1_ref[...] + ada_ref[0, 5:6, :] * y


def _combine(ys, pos, x1, meta, ada3, S):
    T, D = x1.shape
    rt = min(ROW_TILE, S)
    tiles_per_seq = S // rt
    kern = functools.partial(_combine_kernel, rt=rt)
    return pl.pallas_call(
        kern,
        grid=(T // rt,),
        in_specs=[pl.BlockSpec((1, 1, 2 * rt), lambda i: (i, 0, 0), memory_space=pltpu.SMEM),
                  pl.BlockSpec(memory_space=pl.ANY),
                  pl.BlockSpec((rt, D), lambda i: (i, 0)),
                  pl.BlockSpec((rt, LANES), lambda i: (i, 0)),
                  pl.BlockSpec((1, 6, D), lambda i: (i // tiles_per_seq, 0, 0))],
        out_specs=pl.BlockSpec((rt, D), lambda i: (i, 0)),
        out_shape=jax.ShapeDtypeStruct((T, D), jnp.float32),
        scratch_shapes=[pltpu.VMEM((2, rt, D), jnp.float32), pltpu.SemaphoreType.DMA(())],
        compiler_params=_cparams(("arbitrary",)),
        name="combine",
    )(pos.reshape(T // rt, 1, 2 * rt), ys, x1, meta, ada3)


def _routing_plan(meta, counts, T):
    i32 = jnp.int32
    eid = meta[:, _META_EID:_META_EID + 2].astype(i32)
    rank = meta[:, _META_RANK:_META_RANK + 2].astype(i32)
    cnt = counts[0, :N_EXPERTS].astype(i32)
    padded = (cnt + EXPERT_TILE - 1) // EXPERT_TILE * EXPERT_TILE
    ends = jnp.cumsum(padded)
    starts = ends - padded
    pos = starts[eid] + rank
    n_rows = 2 * T + N_EXPERTS * EXPERT_TILE
    n_tiles = n_rows // EXPERT_TILE
    n_used = (ends[-1] // EXPERT_TILE).astype(i32)
    tile_start = jnp.minimum(jnp.arange(n_tiles, dtype=i32), n_used - 1) * EXPERT_TILE
    tile_expert = jnp.sum((ends[None, :] <= tile_start[:, None]).astype(i32), axis=1)
    return pos, starts + cnt, padded - cnt, tile_expert, n_used.reshape(1), n_rows


def kernel(x, c, w_ada, b_ada, norm1_g, w_in, q_norm_g, k_norm_g, w_pool, pool_scale,
           w_out, norm2_g, w_grp, w_exp, w1, w3, w2):
    B, S, D = x.shape
    T = B * S
    x2d = x.reshape(T, D)
    for l in range(w_ada.shape[0]):
        ada3 = _ada(c, w_ada[l], b_ada[l]).reshape(B, 6, D)
        q, kt, v, qi, kia, kib, wi, p = _inproj(x2d, ada3, norm1_g[l], w_in[l], q_norm_g[l], k_norm_g[l], B, S)
        attn = _dsa(q, qi, wi, kia, kib, kt, v, B, S)
        x1, h2, meta, counts = _mix(attn, p, x2d, ada3, w_pool[l], pool_scale[l], w_out[l], norm2_g[l],
                                    w_grp[l], w_exp[l], B, S)
        pos, pad_start, pad_len, tile_expert, n_used, n_rows = _routing_plan(meta, counts, T)
        xs = _dispatch(h2, pos, pad_start, pad_len, n_used, n_rows)
        ys = _experts(xs, w1[l], w3[l], w2[l], tile_expert, n_used)
        x2d = _combine(ys, pos, x1, meta, ada3, S)
    return x2d.reshape(B, S, D)
```

```python
import functools

import numpy as np
import jax
import jax.numpy as jnp
from jax import lax
from jax.experimental import pallas as pl
from jax.experimental.pallas import tpu as pltpu

N_HEADS = 8
N_KV_HEADS = 2
HEAD_DIM = 128
ATTN_WIDTH = N_HEADS * HEAD_DIM
KV_WIDTH = N_KV_HEADS * HEAD_DIM
ROPE_THETA = 500000.0
ROPE_FRACTION = 4
IDX_HEADS = 16
IDX_DIM = 64
TOPK_MAX = 256
POOL_GROUPS = 4
POOL_WINDOWS = (2, 4, 8, 16)
POOL_GROUP_DIM = 256
POOL_WIDTH = POOL_GROUPS * POOL_GROUP_DIM
N_EXPERT_GROUPS = 4
EXPERTS_PER_GROUP = 8
N_EXPERTS = N_EXPERT_GROUPS * EXPERTS_PER_GROUP
D_EXPERT = 512
NORM_EPS = 1e-6

LANES = 128
VMEM_LIMIT_BYTES = 56 * 1024 * 1024

KEY_TILE = 512
Q_TILE = 128
POOL_HALO = 16
EXPERT_TILE = 256
ROW_TILE = 256

_C_Q = 0
_C_K = _C_Q + ATTN_WIDTH
_C_V = _C_K + KV_WIDTH
_C_QI = _C_V + KV_WIDTH
_C_KI = _C_QI + IDX_HEADS * IDX_DIM
_C_WI = _C_KI + LANES
_C_P = _C_WI + LANES
_C_END = _C_P + POOL_WIDTH

_NEG = -1e30
_LOG2_E = 1.4426950408889634
_INT_MIN = -2 ** 31
_KEY_LOWEST_FINITE = -2 ** 31 + 0x00800000


def _cparams(semantics):
    return pltpu.CompilerParams(dimension_semantics=semantics, vmem_limit_bytes=VMEM_LIMIT_BYTES)


def _ada_kernel(c_ref, w_ref, b_ref, o_ref):
    c = c_ref[...]
    s = c * (1.0 / (1.0 + jnp.exp(-c)))
    o_ref[...] = jnp.dot(s, w_ref[...], preferred_element_type=jnp.float32,
                         precision=lax.Precision.HIGHEST) + b_ref[...]


def _ada(c, w_ada, b_ada):
    B, D = c.shape
    N = w_ada.shape[1]
    tn = 1024
    rows = 8
    c_pad = jnp.zeros((rows, D), jnp.float32).at[:B].set(c)
    out = pl.pallas_call(
        _ada_kernel,
        grid=(N // tn,),
        in_specs=[pl.BlockSpec((rows, D), lambda j: (0, 0)),
                  pl.BlockSpec((D, tn), lambda j: (0, j)),
                  pl.BlockSpec((1, tn), lambda j: (0, j))],
        out_specs=pl.BlockSpec((rows, tn), lambda j: (0, j)),
        out_shape=jax.ShapeDtypeStruct((rows, N), jnp.float32),
        compiler_params=_cparams(("arbitrary",)),
        name="ada",
    )(c_pad, w_ada, b_ada.reshape(1, N))
    return out[:B]


def _rope_tables(S, head_dim):
    rd = head_dim // ROPE_FRACTION
    half = rd // 2
    pos = jnp.arange(S, dtype=jnp.float32)
    inv = jnp.float32(ROPE_THETA) ** (-(jnp.arange(half, dtype=jnp.float32) * 2.0) / rd)
    ang = pos[:, None] * inv[None, :]
    cos, sin = jnp.cos(ang), jnp.sin(ang)
    lane = np.arange(LANES) % head_dim
    fidx = np.where(lane < half, lane, lane - half) % half
    in_lo = jnp.asarray(lane < half)[None, :]
    in_hi = jnp.asarray((lane >= half) & (lane < rd))[None, :]
    cos_l, sin_l = cos[:, fidx], sin[:, fidx]
    c_tab = jnp.where(in_lo | in_hi, cos_l, 1.0)
    s1_tab = jnp.where(in_lo, -sin_l, 0.0)
    s2_tab = jnp.where(in_hi, sin_l, 0.0)
    return c_tab, s1_tab, s2_tab, half


def _rope(x, c_tab, s1_tab, s2_tab, half):
    return (x * c_tab + pltpu.roll(x, LANES - half, 1) * s1_tab + pltpu.roll(x, half, 1) * s2_tab)


def _inproj_kernel(x_ref, ada_ref, g1_ref, w_ref, qg_ref, kg_ref,
                   cq_ref, s1q_ref, s2q_ref, ci_ref, s1i_ref, s2i_ref,
                   q_ref, kt_ref, v_ref, qi_ref, kia_ref, kib_ref, wi_ref, p_ref,
                   *, half_qk, half_idx, n_chunks):
    x = x_ref[...]
    ms = jnp.mean(x * x, axis=-1, keepdims=True)
    xn = x * lax.rsqrt(ms + NORM_EPS) * g1_ref[...]
    h = (xn * (1.0 + ada_ref[0, 1:2, :]) + ada_ref[0, 0:1, :]).astype(jnp.bfloat16)

    def proj(lo, hi):
        return jnp.dot(h, w_ref[:, lo:hi], preferred_element_type=jnp.float32)

    cq, s1q, s2q = cq_ref[...], s1q_ref[...], s2q_ref[...]
    ci, s1i, s2i = ci_ref[...], s1i_ref[...], s2i_ref[...]

    def qk_head(slab, gain):
        m = jnp.mean(slab * slab, axis=-1, keepdims=True)
        y = slab * lax.rsqrt(m + NORM_EPS) * gain
        return _rope(y, cq, s1q, s2q, half_qk)

    attn_scale = HEAD_DIM ** -0.5 * _LOG2_E
    q = proj(_C_Q, _C_K)
    for hd in range(N_HEADS):
        sl = slice(hd * HEAD_DIM, (hd + 1) * HEAD_DIM)
        q_ref[:, sl] = (qk_head(q[:, sl], qg_ref[...]) * attn_scale).astype(jnp.bfloat16)

    k = proj(_C_K, _C_V)
    for g in range(N_KV_HEADS):
        kt = qk_head(k[:, g * HEAD_DIM:(g + 1) * HEAD_DIM], kg_ref[...]).T.astype(jnp.bfloat16)
        for c in range(n_chunks):
            kt_ref[0, g, c] = kt[:, c * KEY_TILE:(c + 1) * KEY_TILE]

    v_ref[...] = proj(_C_V, _C_QI).astype(jnp.bfloat16)

    qi = proj(_C_QI, _C_KI)
    for j in range(IDX_HEADS * IDX_DIM // LANES):
        sl = slice(j * LANES, (j + 1) * LANES)
        qi_ref[:, sl] = _rope(qi[:, sl], ci, s1i, s2i, half_idx).astype(jnp.bfloat16)

    ki = _rope(proj(_C_KI, _C_WI), ci, s1i, s2i, half_idx).T
    ki_swapped = jnp.concatenate([ki[IDX_DIM:], ki[:IDX_DIM]], axis=0)
    for c in range(n_chunks):
        kia_ref[0, c] = ki[:, c * KEY_TILE:(c + 1) * KEY_TILE].astype(jnp.bfloat16)
        kib_ref[0, c] = ki_swapped[:, c * KEY_TILE:(c + 1) * KEY_TILE].astype(jnp.bfloat16)

    idx_scale = (IDX_DIM ** -0.5) * (IDX_HEADS ** -0.5)
    wi_ref[...] = proj(_C_WI, _C_P) * idx_scale
    p_ref[...] = proj(_C_P, _C_END)


def _inproj(x2d, ada3, norm1_g, w_in, q_norm_g, k_norm_g, B, S):
    T, D = x2d.shape
    tm = min(512, S)
    n_chunks = tm // KEY_TILE
    tiles_per_seq = S // tm
    bf = jnp.bfloat16

    offs = np.cumsum((ATTN_WIDTH, KV_WIDTH, KV_WIDTH, IDX_HEADS * IDX_DIM, IDX_DIM, IDX_HEADS))
    o_q, o_k, o_v, o_qi, o_ki, o_wi = [int(o) for o in offs]
    zeros = lambda n: jnp.zeros((D, n), w_in.dtype)
    w_pad = jnp.concatenate([
        w_in[:, :o_qi],
        w_in[:, o_qi:o_ki], zeros(LANES - IDX_DIM),
        w_in[:, o_ki:o_wi], zeros(LANES - IDX_HEADS),
        w_in[:, o_wi:],
    ], axis=1).astype(bf)
    assert w_pad.shape[1] == _C_END

    cq, s1q, s2q, half_qk = _rope_tables(S, HEAD_DIM)
    ci, s1i, s2i, half_idx = _rope_tables(S, IDX_DIM)

    row = lambda w: pl.BlockSpec((tm, w), lambda i: (i, 0))
    tab = pl.BlockSpec((tm, LANES), lambda i: (i % tiles_per_seq, 0))
    const = lambda shape: pl.BlockSpec(shape, lambda i: (0,) * len(shape))
    kern = functools.partial(_inproj_kernel, half_qk=half_qk, half_idx=half_idx, n_chunks=n_chunks)
    return pl.pallas_call(
        kern,
        grid=(T // tm,),
        in_specs=[row(D),
                  pl.BlockSpec((1, 6, D), lambda i: (i // tiles_per_seq, 0, 0)),
                  const((1, D)),
                  pl.BlockSpec((D, _C_END), lambda i: (0, 0), pipeline_mode=pl.Buffered(1)),
                  const((1, HEAD_DIM)), const((1, HEAD_DIM)),
                  tab, tab, tab, tab, tab, tab],
        out_specs=[row(ATTN_WIDTH),
                   pl.BlockSpec((1, N_KV_HEADS, n_chunks, HEAD_DIM, KEY_TILE),
                                lambda i: (i // tiles_per_seq, 0, i % tiles_per_seq, 0, 0)),
                   row(KV_WIDTH),
                   row(IDX_HEADS * IDX_DIM),
                   pl.BlockSpec((1, n_chunks, LANES, KEY_TILE),
                                lambda i: (i // tiles_per_seq, i % tiles_per_seq, 0, 0)),
                   pl.BlockSpec((1, n_chunks, LANES, KEY_TILE),
                                lambda i: (i // tiles_per_seq, i % tiles_per_seq, 0, 0)),
                   row(LANES),
                   row(POOL_WIDTH)],
        out_shape=[jax.ShapeDtypeStruct((T, ATTN_WIDTH), bf),
                   jax.ShapeDtypeStruct((B, N_KV_HEADS, S // KEY_TILE, HEAD_DIM, KEY_TILE), bf),
                   jax.ShapeDtypeStruct((T, KV_WIDTH), bf),
                   jax.ShapeDtypeStruct((T, IDX_HEADS * IDX_DIM), bf),
                   jax.ShapeDtypeStruct((B, S // KEY_TILE, LANES, KEY_TILE), bf),
                   jax.ShapeDtypeStruct((B, S // KEY_TILE, LANES, KEY_TILE), bf),
                   jax.ShapeDtypeStruct((T, LANES), jnp.float32),
                   jax.ShapeDtypeStruct((T, POOL_WIDTH), jnp.float32)],
        compiler_params=_cparams(("arbitrary",)),
        name="inproj",
    )(x2d, ada3, norm1_g.reshape(1, D), w_pad, q_norm_g.reshape(1, HEAD_DIM), k_norm_g.reshape(1, HEAD_DIM),
      cq, s1q, s2q, ci, s1i, s2i)


def _dsa_kernel(q_ref, qi_ref, wi_ref, kia_ref, kib_ref, kt_ref, v_ref, o_ref,
                keys_ref, keyst_ref, wb_ref, m_ref, acc_ref, s_ref, *, tq, n_sel, idx_bits):
    qb = pl.program_id(1)
    n_tiles = (qb * tq + tq + KEY_TILE - 1) // KEY_TILE
    lane_chunks = KEY_TILE // LANES
    rep = N_HEADS // N_KV_HEADS
    f32 = jnp.float32

    qi = qi_ref[...]
    n_slabs = IDX_HEADS * IDX_DIM // LANES
    lhs = jnp.concatenate([qi[:, j * LANES:(j + 1) * LANES] for j in range(n_slabs)], axis=0)
    w = wi_ref[...]
    for hd in range(IDX_HEADS):
        wb_ref[hd] = jnp.broadcast_to(w[:, hd:hd + 1], (tq, LANES))
    t_pos = qb * tq + lax.broadcasted_iota(jnp.int32, (tq, KEY_TILE), 0)
    k_off = lax.broadcasted_iota(jnp.int32, (tq, KEY_TILE), 1)

    def sweep_tiles(body, init=0, group=2):
        shift = group.bit_length() - 1
        n_groups = lax.shift_right_logical(n_tiles, shift)

        def grouped(i, carry):
            for u in range(group):
                carry = body(group * i + u, carry)
            return carry

        carry = lax.fori_loop(0, n_groups, grouped, init)
        return lax.fori_loop(group * n_groups, n_tiles, body, carry)

    def score_tile(j, carry):
        za = jnp.dot(lhs, kia_ref[0, j], preferred_element_type=f32)
        zb = jnp.dot(lhs, kib_ref[0, j], preferred_element_type=f32)
        acc = jnp.zeros((tq, KEY_TILE), f32)
        for s in range(n_slabs):
            wa = jnp.concatenate([wb_ref[2 * s]] * lane_chunks, axis=1)
            wo = jnp.concatenate([wb_ref[2 * s + 1]] * lane_chunks, axis=1)
            acc = acc + wa * jnp.maximum(za[s * tq:(s + 1) * tq], 0.0)
            acc = acc + wo * jnp.maximum(zb[s * tq:(s + 1) * tq], 0.0)
        score = jnp.where(j * KEY_TILE + k_off <= t_pos, acc, -jnp.inf)
        bits = lax.bitcast_convert_type(score, jnp.int32)
        key = bits ^ ((bits >> 31) & 0x7FFFFFFF)
        keys_ref[j] = key
        for cc in range(lane_chunks):
            keyst_ref[j, cc * LANES:(cc + 1) * LANES, :] = key[:, cc * LANES:(cc + 1) * LANES].T
        return carry

    sweep_tiles(score_tile)

    def rows_to_lanes_sum(hit):
        return jnp.sum(hit.reshape(KEY_TILE // 8, 8, tq), axis=0)

    def count_ge(cand):
        def body(j, cnt):
            return cnt + rows_to_lanes_sum(jnp.where(keyst_ref[j] >= cand, 1, 0))
        cnt = sweep_tiles(body, jnp.zeros((8, tq), jnp.int32), group=4)
        return jnp.sum(cnt, axis=0, keepdims=True)

    def bisect(i, thr):
        cand = thr + lax.shift_left(jnp.int32(1), 31 - i)
        return jnp.where(count_ge(cand) >= n_sel, cand, thr)

    thr_t = lax.fori_loop(0, 32, bisect, jnp.full((1, tq), _INT_MIN, jnp.int32))

    def per_query_rows(vec_t):
        rows = jnp.broadcast_to(vec_t, (tq, tq)).T
        return jnp.concatenate([rows] * lane_chunks, axis=1)

    thr_wide = per_query_rows(thr_t)

    tied = (count_ge(thr_t) > n_sel) & (thr_t >= _KEY_LOWEST_FINITE)
    any_tied = jnp.max(jnp.where(tied, 1, 0)) > 0

    @pl.when(any_tied)
    def _():
        need = n_sel - count_ge(thr_t + 1)
        row_idx = lax.broadcasted_iota(jnp.int32, (KEY_TILE, tq), 0)

        def tied_before(limit):
            def body(j, cnt):
                hit = jnp.where(j * KEY_TILE + row_idx < limit, 1, 0)
                return cnt + rows_to_lanes_sum(jnp.where(keyst_ref[j] == thr_t, hit, 0))
            cnt = lax.fori_loop(0, n_tiles, body, jnp.zeros((8, tq), jnp.int32))
            return jnp.sum(cnt, axis=0, keepdims=True)

        def bisect_index(i, last):
            cand = last + lax.shift_left(jnp.int32(1), idx_bits - 1 - i)
            return jnp.where(tied_before(cand) < need, cand, last)

        last_t = lax.fori_loop(0, idx_bits, bisect_index, jnp.zeros((1, tq), jnp.int32))
        last_wide = per_query_rows(last_t)

        def demote(j, carry):
            kk = keys_ref[j]
            drop = jnp.where(j * KEY_TILE + k_off > last_wide, thr_wide - 1, kk)
            keys_ref[j] = jnp.where(kk == thr_wide, drop, kk)
            return carry

        lax.fori_loop(0, n_tiles, demote, 0)

    thr_wide = jnp.maximum(thr_wide, _KEY_LOWEST_FINITE)

    q = q_ref[...]
    ones_col = jnp.where(lax.broadcasted_iota(jnp.int32, (KEY_TILE, HEAD_DIM), 1) == 0, 1.0, 0.0
                         ).astype(jnp.bfloat16)
    for g in range(N_KV_HEADS):
        qg = jnp.concatenate([q[:, (g * rep + r) * HEAD_DIM:(g * rep + r + 1) * HEAD_DIM] for r in range(rep)],
                             axis=0)
        m_ref[...] = jnp.full(m_ref.shape, _NEG, f32)
        acc_ref[...] = jnp.zeros(acc_ref.shape, f32)

        def scores(j, carry, qg=qg, g=g):
            bias = jnp.where(keys_ref[j] >= thr_wide, 0.0, _NEG)
            s = jnp.dot(qg, kt_ref[0, g, j], preferred_element_type=f32) + jnp.concatenate([bias] * rep, axis=0)
            s_ref[j] = s
            m_part = m_ref[...]
            for cc in range(lane_chunks):
                m_part = jnp.maximum(m_part, s[:, cc * LANES:(cc + 1) * LANES])
            m_ref[...] = m_part
            return carry

        sweep_tiles(scores)
        m_ref[...] = jnp.broadcast_to(jnp.max(m_ref[...], axis=-1, keepdims=True), m_ref.shape)

        def values(j, carry, g=g):
            m_row = m_ref[...]
            p = jnp.exp2(s_ref[j] - jnp.concatenate([m_row] * lane_chunks, axis=1)).astype(jnp.bfloat16)
            start = pl.multiple_of(j * KEY_TILE, KEY_TILE)
            vt = jnp.concatenate([v_ref[pl.ds(start, KEY_TILE), g * HEAD_DIM:(g + 1) * HEAD_DIM], ones_col],
                                 axis=1)
            acc_ref[...] += jnp.dot(p, vt, preferred_element_type=f32)
            return carry

        sweep_tiles(values)
        acc = acc_ref[...]
        o = acc[:, :HEAD_DIM] / acc[:, HEAD_DIM:HEAD_DIM + 1]
        for r in range(rep):
            hd = g * rep + r
            o_ref[:, hd * HEAD_DIM:(hd + 1) * HEAD_DIM] = o[r * tq:(r + 1) * tq].astype(o_ref.dtype)


def _dsa(q, qi, wi, kia, kib, kt, v, B, S):
    tq = min(Q_TILE, S)
    n_sel = min(TOPK_MAX, S // 4)
    nq = S // tq
    n_kt = S // KEY_TILE
    rep = N_HEADS // N_KV_HEADS
    row = lambda w: pl.BlockSpec((tq, w), lambda b, i: (b * nq + i, 0))
    once = pl.Buffered(1)
    kern = functools.partial(_dsa_kernel, tq=tq, n_sel=n_sel, idx_bits=max(1, (S - 1).bit_length()))
    return pl.pallas_call(
        kern,
        grid=(B, nq),
        in_specs=[row(ATTN_WIDTH), row(IDX_HEADS * IDX_DIM), row(LANES),
                  pl.BlockSpec((1, n_kt, LANES, KEY_TILE), lambda b, i: (b, 0, 0, 0), pipeline_mode=once),
                  pl.BlockSpec((1, n_kt, LANES, KEY_TILE), lambda b, i: (b, 0, 0, 0), pipeline_mode=once),
                  pl.BlockSpec((1, N_KV_HEADS, n_kt, HEAD_DIM, KEY_TILE), lambda b, i: (b, 0, 0, 0, 0),
                               pipeline_mode=once),
                  pl.BlockSpec((S, KV_WIDTH), lambda b, i: (b, 0), pipeline_mode=once)],
        out_specs=row(ATTN_WIDTH),
        out_shape=jax.ShapeDtypeStruct((B * S, ATTN_WIDTH), jnp.bfloat16),
        scratch_shapes=[pltpu.VMEM((n_kt, tq, KEY_TILE), jnp.int32),
                        pltpu.VMEM((n_kt, KEY_TILE, tq), jnp.int32),
                        pltpu.VMEM((IDX_HEADS, tq, LANES), jnp.float32),
                        pltpu.VMEM((rep * tq, LANES), jnp.float32),
                        pltpu.VMEM((rep * tq, 2 * HEAD_DIM), jnp.float32),
                        pltpu.VMEM((n_kt, rep * tq, KEY_TILE), jnp.float32)],
        compiler_params=_cparams(("arbitrary", "arbitrary")),
        name="dsa",
    )(q, qi, wi, kia, kib, kt, v)


_META_EID, _META_W, _META_RANK = 0, 2, 4


def _mix_kernel(attn_ref, p_ref, halo_ref, x_ref, ada_ref, wpool_ref, pscale_ref, wout_ref, g2_ref, wr_ref,
                x1_ref, h2_ref, meta_ref, counts_ref, pbuf_ref, run_ref, *, tm, tiles_per_seq):
    i = pl.program_id(0)
    f32 = jnp.float32
    bf = jnp.bfloat16
    seq_tile = i % tiles_per_seq

    @pl.when(i == 0)
    def _():
        run_ref[...] = jnp.zeros(run_ref.shape, f32)

    p = p_ref[...]
    pbuf_ref[0:POOL_HALO, :] = jnp.where(seq_tile == 0, 0.0, halo_ref[...])
    pbuf_ref[POOL_HALO:POOL_HALO + tm, :] = p
    t_pos = seq_tile * tm + lax.broadcasted_iota(jnp.int32, (tm, 1), 0)
    pieces = [attn_ref[...]]
    for g, win in enumerate(POOL_WINDOWS):
        cs = slice(g * POOL_GROUP_DIM, (g + 1) * POOL_GROUP_DIM)
        tot = p[:, cs]
        for back in range(1, win):
            tot = tot + pbuf_ref[POOL_HALO - back:POOL_HALO - back + tm, cs]
        count = jnp.minimum(t_pos + 1, win).astype(f32)
        mixed = tot / count - p[:, cs]
        y = jnp.dot(mixed.astype(bf), wpool_ref[g], preferred_element_type=f32) * pscale_ref[:, cs]
        pieces.append(y.astype(bf))
    mix = jnp.dot(jnp.concatenate(pieces, axis=1), wout_ref[...], preferred_element_type=f32)
    x1 = x_ref[...] + ada_ref[0, 2:3, :] * mix
    x1_ref[...] = x1

    ms = jnp.mean(x1 * x1, axis=-1, keepdims=True)
    h2 = (x1 * lax.rsqrt(ms + NORM_EPS) * g2_ref[...]) * (1.0 + ada_ref[0, 4:5, :]) + ada_ref[0, 3:4, :]
    h2_ref[...] = h2

    logits = jnp.dot(h2, wr_ref[...], preferred_element_type=f32, precision=lax.Precision.HIGHEST)
    lane = lax.broadcasted_iota(jnp.int32, (tm, LANES), 1)
    big = jnp.int32(LANES)
    rmax = lambda a: jnp.max(a, axis=-1, keepdims=True)
    rmin = lambda a: jnp.min(a, axis=-1, keepdims=True)
    rsum = lambda a: jnp.sum(a, axis=-1, keepdims=True)
    is_grp = lane < N_EXPERT_GROUPS
    m_g = rmax(jnp.where(is_grp, logits, -jnp.inf))
    p_g = 1.0 / rsum(jnp.where(is_grp, jnp.exp(logits - m_g), 0.0))
    g_sel = rmin(jnp.where(is_grp & (logits == m_g), lane, big))
    lo = N_EXPERT_GROUPS + EXPERTS_PER_GROUP * g_sel
    in_grp = (lane >= lo) & (lane < lo + EXPERTS_PER_GROUP)
    le = jnp.where(in_grp, logits, -jnp.inf)
    m_1 = rmax(le)
    i_1 = rmin(jnp.where(le == m_1, lane, big))
    le2 = jnp.where(lane == i_1, -jnp.inf, le)
    m_2 = rmax(le2)
    i_2 = rmin(jnp.where(le2 == m_2, lane, big))
    e_2 = jnp.exp(m_2 - m_1)
    w_1 = p_g / (1.0 + e_2)
    w_2 = p_g * e_2 / (1.0 + e_2)
    eid_1 = i_1 - N_EXPERT_GROUPS
    eid_2 = i_2 - N_EXPERT_GROUPS

    oh1 = lane == eid_1
    oh2 = lane == eid_2
    onehot = jnp.where(oh1, 1.0, 0.0) + jnp.where(oh2, 1.0, 0.0)
    r_i = lax.broadcasted_iota(jnp.int32, (tm, tm), 0)
    c_i = lax.broadcasted_iota(jnp.int32, (tm, tm), 1)
    earlier = jnp.where(c_i < r_i, 1.0, 0.0).astype(bf)
    before = jnp.dot(earlier, onehot.astype(bf), preferred_element_type=f32) + run_ref[...]
    rank_1 = rsum(jnp.where(oh1, before, 0.0))
    rank_2 = rsum(jnp.where(oh2, before, 0.0))
    run_ref[...] = run_ref[...] + jnp.sum(onehot, axis=0, keepdims=True)
    counts_ref[...] = run_ref[...]

    meta = jnp.zeros((tm, LANES), f32)
    for off, val in ((_META_EID, eid_1.astype(f32)), (_META_EID + 1, eid_2.astype(f32)),
                     (_META_W, w_1), (_META_W + 1, w_2), (_META_RANK, rank_1), (_META_RANK + 1, rank_2)):
        meta = jnp.where(lane == off, val, meta)
    meta_ref[...] = meta


def _mix(attn, p, x2d, ada3, w_pool, pool_scale, w_out, norm2_g, w_grp, w_exp, B, S):
    T, D = x2d.shape
    tm = min(256, S)
    tiles_per_seq = S // tm
    halo_blocks = tm // POOL_HALO
    bf = jnp.bfloat16
    w_router = jnp.concatenate(
        [w_grp, w_exp, jnp.zeros((D, LANES - N_EXPERT_GROUPS - N_EXPERTS), w_grp.dtype)], axis=1)
    row = lambda w: pl.BlockSpec((tm, w), lambda i: (i, 0))
    const = lambda shape: pl.BlockSpec(shape, lambda i: (0,) * len(shape))
    once = lambda shape: pl.BlockSpec(shape, lambda i: (0,) * len(shape), pipeline_mode=pl.Buffered(1))
    kern = functools.partial(_mix_kernel, tm=tm, tiles_per_seq=tiles_per_seq)
    return pl.pallas_call(
        kern,
        grid=(T // tm,),
        in_specs=[row(ATTN_WIDTH), row(POOL_WIDTH),
                  pl.BlockSpec((POOL_HALO, POOL_WIDTH), lambda i: (jnp.maximum(i * halo_blocks - 1, 0), 0)),
                  row(D),
                  pl.BlockSpec((1, 6, D), lambda i: (i // tiles_per_seq, 0, 0)),
                  once((POOL_GROUPS, POOL_GROUP_DIM, POOL_GROUP_DIM)), const((1, POOL_WIDTH)),
                  once((ATTN_WIDTH + POOL_WIDTH, D)), const((1, D)), once((D, LANES))],
        out_specs=[row(D), row(D), row(LANES), const((1, LANES))],
        out_shape=[jax.ShapeDtypeStruct((T, D), jnp.float32),
                   jax.ShapeDtypeStruct((T, D), jnp.float32),
                   jax.ShapeDtypeStruct((T, LANES), jnp.float32),
                   jax.ShapeDtypeStruct((1, LANES), jnp.float32)],
        scratch_shapes=[pltpu.VMEM((POOL_HALO + tm, POOL_WIDTH), jnp.float32),
                        pltpu.VMEM((1, LANES), jnp.float32)],
        compiler_params=_cparams(("arbitrary",)),
        name="mix",
    )(attn, p, p, x2d, ada3, w_pool.astype(bf), pool_scale.reshape(1, POOL_WIDTH), w_out.astype(bf),
      norm2_g.reshape(1, D), w_router)


def _dispatch_kernel(pad_start_ref, pad_len_ref, n_used_ref, pos_ref, h2_ref, xs_ref, zeros_ref, sem, *, rt, n_tiles):
    i = pl.program_id(0)

    @pl.when(i == 0)
    def _():
        zeros_ref[...] = jnp.zeros(zeros_ref.shape, zeros_ref.dtype)

        def zero_row(r):
            return pltpu.make_async_copy(zeros_ref.at[pl.ds(0, 1)], xs_ref.at[pl.ds(r, 1)], sem)

        def zero_tile(t):
            start = pl.multiple_of(t * EXPERT_TILE, EXPERT_TILE)
            return pltpu.make_async_copy(zeros_ref, xs_ref.at[pl.ds(start, EXPERT_TILE)], sem)

        def fill(e, carry):
            start = pad_start_ref[e]
            n = pad_len_ref[e]
            lax.fori_loop(0, n, lambda r, c: (zero_row(start + r).start(), c)[1], 0)
            lax.fori_loop(0, n, lambda r, c: (zero_row(0).wait(), c)[1], 0)
            return carry

        lax.fori_loop(0, N_EXPERTS, fill, 0)
        first_unused = n_used_ref[0]
        lax.fori_loop(first_unused, n_tiles, lambda t, c: (zero_tile(t).start(), c)[1], 0)
        lax.fori_loop(first_unused, n_tiles, lambda t, c: (zero_tile(0).wait(), c)[1], 0)

    def row_copy(t, slot):
        return pltpu.make_async_copy(h2_ref.at[pl.ds(t, 1)], xs_ref.at[pl.ds(pos_ref[0, 0, 2 * t + slot], 1)], sem)

    def issue(t, carry):
        row_copy(t, 0).start(priority=0)
        row_copy(t, 1).start(priority=1)
        return carry

    def drain(t, carry):
        for _ in range(2):
            pltpu.make_async_copy(h2_ref.at[pl.ds(0, 1)], xs_ref.at[pl.ds(0, 1)], sem).wait()
        return carry

    lax.fori_loop(0, rt, issue, 0)
    lax.fori_loop(0, rt, drain, 0)


def _dispatch(h2, pos, pad_start, pad_len, n_used, n_rows):
    T, D = h2.shape
    rt = min(ROW_TILE, T)
    kern = functools.partial(_dispatch_kernel, rt=rt, n_tiles=n_rows // EXPERT_TILE)
    return pl.pallas_call(
        kern,
        grid_spec=pltpu.PrefetchScalarGridSpec(
            num_scalar_prefetch=3,
            grid=(T // rt,),
            in_specs=[pl.BlockSpec((1, 1, 2 * rt), lambda i, *_: (i, 0, 0), memory_space=pltpu.SMEM),
                      pl.BlockSpec((rt, D), lambda i, *_: (i, 0))],
            out_specs=pl.BlockSpec(memory_space=pl.ANY),
            scratch_shapes=[pltpu.VMEM((EXPERT_TILE, D), h2.dtype), pltpu.SemaphoreType.DMA(())]),
        out_shape=jax.ShapeDtypeStruct((n_rows, D), h2.dtype),
        compiler_params=_cparams(("arbitrary",)),
        name="dispatch",
    )(pad_start, pad_len, n_used, pos.reshape(T // rt, 1, 2 * rt), h2)


def _experts_kernel(tile_expert_ref, n_used_ref, xs_ref, w1_ref, w3_ref, w2_ref, ys_ref, w1b_ref, w3b_ref, w2b_ref):
    i = pl.program_id(0)
    bf = jnp.bfloat16

    @pl.when((i == 0) | (tile_expert_ref[i] != tile_expert_ref[jnp.maximum(i - 1, 0)]))
    def _():
        w1b_ref[...] = w1_ref[0].astype(bf)
        w3b_ref[...] = w3_ref[0].astype(bf)
        w2b_ref[...] = w2_ref[0].astype(bf)

    @pl.when(i < n_used_ref[0])
    def _():
        xb = xs_ref[...].astype(bf)
        a1 = jnp.dot(xb, w1b_ref[...], preferred_element_type=jnp.float32)
        a3 = jnp.dot(xb, w3b_ref[...], preferred_element_type=jnp.float32)
        act = (a1 * (1.0 / (1.0 + jnp.exp(-a1))) * a3).astype(bf)
        ys_ref[...] = jnp.dot(act, w2b_ref[...], preferred_element_type=jnp.float32)

    @pl.when(i >= n_used_ref[0])
    def _():
        ys_ref[...] = jnp.zeros(ys_ref.shape, ys_ref.dtype)


def _experts(xs, w1, w3, w2, tile_expert, n_used):
    n_rows, D = xs.shape
    n_tiles = n_rows // EXPERT_TILE
    bf = jnp.bfloat16
    return pl.pallas_call(
        _experts_kernel,
        grid_spec=pltpu.PrefetchScalarGridSpec(
            num_scalar_prefetch=2,
            grid=(n_tiles,),
            in_specs=[pl.BlockSpec((EXPERT_TILE, D), lambda i, te, nu: (jnp.minimum(i, nu[0] - 1), 0)),
                      pl.BlockSpec((1, D, D_EXPERT), lambda i, te, nu: (te[i], 0, 0)),
                      pl.BlockSpec((1, D, D_EXPERT), lambda i, te, nu: (te[i], 0, 0)),
                      pl.BlockSpec((1, D_EXPERT, D), lambda i, te, nu: (te[i], 0, 0))],
            out_specs=pl.BlockSpec((EXPERT_TILE, D), lambda i, te, nu: (i, 0)),
            scratch_shapes=[pltpu.VMEM((D, D_EXPERT), bf), pltpu.VMEM((D, D_EXPERT), bf),
                            pltpu.VMEM((D_EXPERT, D), bf)]),
        out_shape=jax.ShapeDtypeStruct((n_rows, D), jnp.float32),
        compiler_params=_cparams(("arbitrary",)),
        name="experts",
    )(tile_expert, n_used, xs, w1, w3, w2)


def _combine_kernel(pos_ref, ys_ref, x1_ref, meta_ref, ada_ref, o_ref, rows_ref, sem, *, rt):
    def row_copy(t, slot):
        return pltpu.make_async_copy(ys_ref.at[pl.ds(pos_ref[0, 0, 2 * t + slot], 1)],
                                     rows_ref.at[slot, pl.ds(t, 1)], sem)

    def issue(t, carry):
        row_copy(t, 0).start(priority=0)
        row_copy(t, 1).start(priority=1)
        return carry

    def drain(t, carry):
        for slot in range(2):
            pltpu.make_async_copy(ys_ref.at[pl.ds(0, 1)], rows_ref.at[slot, pl.ds(0, 1)], sem).wait()
        return carry

    lax.fori_loop(0, rt, issue, 0)
    lax.fori_loop(0, rt, drain, 0)
    meta = meta_ref[...]
    y = meta[:, _META_W:_META_W + 1] * rows_ref[0] + meta[:, _META_W + 1:_META_W + 2] * rows_ref[1]
    o_ref[...] = x1_ref[...] + ada_ref[0, 5:6, :] * y


def _combine(ys, pos, x1, meta, ada3, S):
    T, D = x1.shape
    rt = min(ROW_TILE, S)
    tiles_per_seq = S // rt
    kern = functools.partial(_combine_kernel, rt=rt)
    return pl.pallas_call(
        kern,
        grid=(T // rt,),
        in_specs=[pl.BlockSpec((1, 1, 2 * rt), lambda i: (i, 0, 0), memory_space=pltpu.SMEM),
                  pl.BlockSpec(memory_space=pl.ANY),
                  pl.BlockSpec((rt, D), lambda i: (i, 0)),
                  pl.BlockSpec((rt, LANES), lambda i: (i, 0)),
                  pl.BlockSpec((1, 6, D), lambda i: (i // tiles_per_seq, 0, 0))],
        out_specs=pl.BlockSpec((rt, D), lambda i: (i, 0)),
        out_shape=jax.ShapeDtypeStruct((T, D), jnp.float32),
        scratch_shapes=[pltpu.VMEM((2, rt, D), jnp.float32), pltpu.SemaphoreType.DMA(())],
        compiler_params=_cparams(("arbitrary",)),
        name="combine",
    )(pos.reshape(T // rt, 1, 2 * rt), ys, x1, meta, ada3)


def _routing_plan(meta, counts, T):
    i32 = jnp.int32
    eid = meta[:, _META_EID:_META_EID + 2].reshape(-1).astype(i32)
    rank = meta[:, _META_RANK:_META_RANK + 2].reshape(-1).astype(i32)
    cnt = counts[0, :N_EXPERTS].astype(i32)
    padded = (cnt + EXPERT_TILE - 1) // EXPERT_TILE * EXPERT_TILE
    ends = jnp.cumsum(padded)
    starts = ends - padded
    pos = starts[eid] + rank
    n_rows = 2 * T + N_EXPERTS * EXPERT_TILE
    n_tiles = n_rows // EXPERT_TILE
    n_used = (ends[-1] // EXPERT_TILE).astype(i32)
    tile_start = jnp.minimum(jnp.arange(n_tiles, dtype=i32), n_used - 1) * EXPERT_TILE
    tile_expert = jnp.sum((ends[None, :] <= tile_start[:, None]).astype(i32), axis=1)
    return pos, starts + cnt, padded - cnt, tile_expert, n_used.reshape(1), n_rows


def kernel(x, c, w_ada, b_ada, norm1_g, w_in, q_norm_g, k_norm_g, w_pool, pool_scale,
           w_out, norm2_g, w_grp, w_exp, w1, w3, w2):
    B, S, D = x.shape
    T = B * S
    x2d = x.reshape(T, D)
    for l in range(w_ada.shape[0]):
        ada3 = _ada(c, w_ada[l], b_ada[l]).reshape(B, 6, D)
        q, kt, v, qi, kia, kib, wi, p = _inproj(x2d, ada3, norm1_g[l], w_in[l], q_norm_g[l], k_norm_g[l], B, S)
        attn = _dsa(q, qi, wi, kia, kib, kt, v, B, S)
        x1, h2, meta, counts = _mix(attn, p, x2d, ada3, w_pool[l], pool_scale[l], w_out[l], norm2_g[l],
                                    w_grp[l], w_exp[l], B, S)
        pos, pad_start, pad_len, tile_expert, n_used, n_rows = _routing_plan(meta, counts, T)
        xs = _dispatch(h2, pos, pad_start, pad_len, n_used, n_rows)
        ys = _experts(xs, w1[l], w3[l], w2[l], tile_expert, n_used)
        x2d = _combine(ys, pos, x1, meta, ada3, S)
    return x2d.reshape(B, S, D)
```

```python
import functools

import numpy as np
import jax
import jax.numpy as jnp
from jax import lax
from jax.experimental import pallas as pl
from jax.experimental.pallas import tpu as pltpu

N_HEADS = 8
N_KV_HEADS = 2
HEAD_DIM = 128
ATTN_WIDTH = N_HEADS * HEAD_DIM
KV_WIDTH = N_KV_HEADS * HEAD_DIM
ROPE_THETA = 500000.0
ROPE_FRACTION = 4
IDX_HEADS = 16
IDX_DIM = 64
TOPK_MAX = 256
POOL_GROUPS = 4
POOL_WINDOWS = (2, 4, 8, 16)
POOL_GROUP_DIM = 256
POOL_WIDTH = POOL_GROUPS * POOL_GROUP_DIM
N_EXPERT_GROUPS = 4
EXPERTS_PER_GROUP = 8
N_EXPERTS = N_EXPERT_GROUPS * EXPERTS_PER_GROUP
D_EXPERT = 512
NORM_EPS = 1e-6

LANES = 128
VMEM_LIMIT_BYTES = 56 * 1024 * 1024

KEY_TILE = 512
Q_TILE = 128
POOL_HALO = 16
EXPERT_TILE = 256
ROW_TILE = 256

_C_Q = 0
_C_K = _C_Q + ATTN_WIDTH
_C_V = _C_K + KV_WIDTH
_C_QI = _C_V + KV_WIDTH
_C_KI = _C_QI + IDX_HEADS * IDX_DIM
_C_WI = _C_KI + LANES
_C_P = _C_WI + LANES
_C_END = _C_P + POOL_WIDTH

_NEG = -1e30
_LOG2_E = 1.4426950408889634
_INT_MIN = -2 ** 31
_KEY_LOWEST_FINITE = -2 ** 31 + 0x00800000


def _cparams(semantics):
    return pltpu.CompilerParams(dimension_semantics=semantics, vmem_limit_bytes=VMEM_LIMIT_BYTES)


def _ada_kernel(c_ref, w_ref, b_ref, o_ref):
    c = c_ref[...]
    s = c * (1.0 / (1.0 + jnp.exp(-c)))
    o_ref[...] = jnp.dot(s, w_ref[...], preferred_element_type=jnp.float32,
                         precision=lax.Precision.HIGHEST) + b_ref[...]


def _ada(c, w_ada, b_ada):
    B, D = c.shape
    N = w_ada.shape[1]
    tn = 1024
    rows = 8
    c_pad = jnp.zeros((rows, D), jnp.float32).at[:B].set(c)
    out = pl.pallas_call(
        _ada_kernel,
        grid=(N // tn,),
        in_specs=[pl.BlockSpec((rows, D), lambda j: (0, 0)),
                  pl.BlockSpec((D, tn), lambda j: (0, j)),
                  pl.BlockSpec((1, tn), lambda j: (0, j))],
        out_specs=pl.BlockSpec((rows, tn), lambda j: (0, j)),
        out_shape=jax.ShapeDtypeStruct((rows, N), jnp.float32),
        compiler_params=_cparams(("arbitrary",)),
        name="ada",
    )(c_pad, w_ada, b_ada.reshape(1, N))
    return out[:B]


def _rope_tables(S, head_dim):
    rd = head_dim // ROPE_FRACTION
    half = rd // 2
    pos = jnp.arange(S, dtype=jnp.float32)
    inv = jnp.float32(ROPE_THETA) ** (-(jnp.arange(half, dtype=jnp.float32) * 2.0) / rd)
    ang = pos[:, None] * inv[None, :]
    cos, sin = jnp.cos(ang), jnp.sin(ang)
    lane = np.arange(LANES) % head_dim
    fidx = np.where(lane < half, lane, lane - half) % half
    in_lo = jnp.asarray(lane < half)[None, :]
    in_hi = jnp.asarray((lane >= half) & (lane < rd))[None, :]
    cos_l, sin_l = cos[:, fidx], sin[:, fidx]
    c_tab = jnp.where(in_lo | in_hi, cos_l, 1.0)
    s1_tab = jnp.where(in_lo, -sin_l, 0.0)
    s2_tab = jnp.where(in_hi, sin_l, 0.0)
    return c_tab, s1_tab, s2_tab, half


def _rope(x, c_tab, s1_tab, s2_tab, half):
    return (x * c_tab + pltpu.roll(x, LANES - half, 1) * s1_tab + pltpu.roll(x, half, 1) * s2_tab)


def _inproj_kernel(x_ref, ada_ref, g1_ref, w_ref, qg_ref, kg_ref,
                   cq_ref, s1q_ref, s2q_ref, ci_ref, s1i_ref, s2i_ref,
                   q_ref, kt_ref, v_ref, qi_ref, kia_ref, kib_ref, wi_ref, p_ref,
                   *, half_qk, half_idx, n_chunks):
    x = x_ref[...]
    ms = jnp.mean(x * x, axis=-1, keepdims=True)
    xn = x * lax.rsqrt(ms + NORM_EPS) * g1_ref[...]
    h = (xn * (1.0 + ada_ref[0, 1:2, :]) + ada_ref[0, 0:1, :]).astype(jnp.bfloat16)

    def proj(lo, hi):
        return jnp.dot(h, w_ref[:, lo:hi], preferred_element_type=jnp.float32)

    cq, s1q, s2q = cq_ref[...], s1q_ref[...], s2q_ref[...]
    ci, s1i, s2i = ci_ref[...], s1i_ref[...], s2i_ref[...]

    def qk_head(slab, gain):
        m = jnp.mean(slab * slab, axis=-1, keepdims=True)
        y = slab * lax.rsqrt(m + NORM_EPS) * gain
        return _rope(y, cq, s1q, s2q, half_qk)

    attn_scale = HEAD_DIM ** -0.5 * _LOG2_E
    q = proj(_C_Q, _C_K)
    for hd in range(N_HEADS):
        sl = slice(hd * HEAD_DIM, (hd + 1) * HEAD_DIM)
        q_ref[:, sl] = (qk_head(q[:, sl], qg_ref[...]) * attn_scale).astype(jnp.bfloat16)

    k = proj(_C_K, _C_V)
    for g in range(N_KV_HEADS):
        kt = qk_head(k[:, g * HEAD_DIM:(g + 1) * HEAD_DIM], kg_ref[...]).T.astype(jnp.bfloat16)
        for c in range(n_chunks):
            kt_ref[0, g, c] = kt[:, c * KEY_TILE:(c + 1) * KEY_TILE]

    v_ref[...] = proj(_C_V, _C_QI).astype(jnp.bfloat16)

    qi = proj(_C_QI, _C_KI)
    for j in range(IDX_HEADS * IDX_DIM // LANES):
        sl = slice(j * LANES, (j + 1) * LANES)
        qi_ref[:, sl] = _rope(qi[:, sl], ci, s1i, s2i, half_idx).astype(jnp.bfloat16)

    ki = _rope(proj(_C_KI, _C_WI), ci, s1i, s2i, half_idx).T
    ki_swapped = jnp.concatenate([ki[IDX_DIM:], ki[:IDX_DIM]], axis=0)
    for c in range(n_chunks):
        kia_ref[0, c] = ki[:, c * KEY_TILE:(c + 1) * KEY_TILE].astype(jnp.bfloat16)
        kib_ref[0, c] = ki_swapped[:, c * KEY_TILE:(c + 1) * KEY_TILE].astype(jnp.bfloat16)

    idx_scale = (IDX_DIM ** -0.5) * (IDX_HEADS ** -0.5)
    wi_ref[...] = proj(_C_WI, _C_P) * idx_scale
    p_ref[...] = proj(_C_P, _C_END)


def _inproj(x2d, ada3, norm1_g, w_in, q_norm_g, k_norm_g, B, S):
    T, D = x2d.shape
    tm = min(512, S)
    n_chunks = tm // KEY_TILE
    tiles_per_seq = S // tm
    bf = jnp.bfloat16

    offs = np.cumsum((ATTN_WIDTH, KV_WIDTH, KV_WIDTH, IDX_HEADS * IDX_DIM, IDX_DIM, IDX_HEADS))
    o_q, o_k, o_v, o_qi, o_ki, o_wi = [int(o) for o in offs]
    zeros = lambda n: jnp.zeros((D, n), w_in.dtype)
    w_pad = jnp.concatenate([
        w_in[:, :o_qi],
        w_in[:, o_qi:o_ki], zeros(LANES - IDX_DIM),
        w_in[:, o_ki:o_wi], zeros(LANES - IDX_HEADS),
        w_in[:, o_wi:],
    ], axis=1).astype(bf)
    assert w_pad.shape[1] == _C_END

    cq, s1q, s2q, half_qk = _rope_tables(S, HEAD_DIM)
    ci, s1i, s2i, half_idx = _rope_tables(S, IDX_DIM)

    row = lambda w: pl.BlockSpec((tm, w), lambda i: (i, 0))
    tab = pl.BlockSpec((tm, LANES), lambda i: (i % tiles_per_seq, 0))
    const = lambda shape: pl.BlockSpec(shape, lambda i: (0,) * len(shape))
    kern = functools.partial(_inproj_kernel, half_qk=half_qk, half_idx=half_idx, n_chunks=n_chunks)
    return pl.pallas_call(
        kern,
        grid=(T // tm,),
        in_specs=[row(D),
                  pl.BlockSpec((1, 6, D), lambda i: (i // tiles_per_seq, 0, 0)),
                  const((1, D)),
                  pl.BlockSpec((D, _C_END), lambda i: (0, 0), pipeline_mode=pl.Buffered(1)),
                  const((1, HEAD_DIM)), const((1, HEAD_DIM)),
                  tab, tab, tab, tab, tab, tab],
        out_specs=[row(ATTN_WIDTH),
                   pl.BlockSpec((1, N_KV_HEADS, n_chunks, HEAD_DIM, KEY_TILE),
                                lambda i: (i // tiles_per_seq, 0, i % tiles_per_seq, 0, 0)),
                   row(KV_WIDTH),
                   row(IDX_HEADS * IDX_DIM),
                   pl.BlockSpec((1, n_chunks, LANES, KEY_TILE),
                                lambda i: (i // tiles_per_seq, i % tiles_per_seq, 0, 0)),
                   pl.BlockSpec((1, n_chunks, LANES, KEY_TILE),
                                lambda i: (i // tiles_per_seq, i % tiles_per_seq, 0, 0)),
                   row(LANES),
                   row(POOL_WIDTH)],
        out_shape=[jax.ShapeDtypeStruct((T, ATTN_WIDTH), bf),
                   jax.ShapeDtypeStruct((B, N_KV_HEADS, S // KEY_TILE, HEAD_DIM, KEY_TILE), bf),
                   jax.ShapeDtypeStruct((T, KV_WIDTH), bf),
                   jax.ShapeDtypeStruct((T, IDX_HEADS * IDX_DIM), bf),
                   jax.ShapeDtypeStruct((B, S // KEY_TILE, LANES, KEY_TILE), bf),
                   jax.ShapeDtypeStruct((B, S // KEY_TILE, LANES, KEY_TILE), bf),
                   jax.ShapeDtypeStruct((T, LANES), jnp.float32),
                   jax.ShapeDtypeStruct((T, POOL_WIDTH), jnp.float32)],
        compiler_params=_cparams(("arbitrary",)),
        name="inproj",
    )(x2d, ada3, norm1_g.reshape(1, D), w_pad, q_norm_g.reshape(1, HEAD_DIM), k_norm_g.reshape(1, HEAD_DIM),
      cq, s1q, s2q, ci, s1i, s2i)


def _dsa_kernel(q_ref, qi_ref, wi_ref, kia_ref, kib_ref, kt_ref, v_ref, o_ref,
                keys_ref, keyst_ref, wb_ref, m_ref, acc_ref, s_ref, *, tq, n_sel, idx_bits):
    qb = pl.program_id(1)
    n_tiles = (qb * tq + tq + KEY_TILE - 1) // KEY_TILE
    lane_chunks = KEY_TILE // LANES
    rep = N_HEADS // N_KV_HEADS
    f32 = jnp.float32

    qi = qi_ref[...]
    n_slabs = IDX_HEADS * IDX_DIM // LANES
    lhs = jnp.concatenate([qi[:, j * LANES:(j + 1) * LANES] for j in range(n_slabs)], axis=0)
    w = wi_ref[...]
    for hd in range(IDX_HEADS):
        wb_ref[hd] = jnp.broadcast_to(w[:, hd:hd + 1], (tq, LANES))
    t_pos = qb * tq + lax.broadcasted_iota(jnp.int32, (tq, KEY_TILE), 0)
    k_off = lax.broadcasted_iota(jnp.int32, (tq, KEY_TILE), 1)

    def sweep_tiles(body, init=0, group=2):
        shift = group.bit_length() - 1
        n_groups = lax.shift_right_logical(n_tiles, shift)

        def grouped(i, carry):
            for u in range(group):
                carry = body(group * i + u, carry)
            return carry

        carry = lax.fori_loop(0, n_groups, grouped, init)
        return lax.fori_loop(group * n_groups, n_tiles, body, carry)

    def score_tile(j, carry):
        za = jnp.dot(lhs, kia_ref[0, j], preferred_element_type=f32)
        zb = jnp.dot(lhs, kib_ref[0, j], preferred_element_type=f32)
        acc = jnp.zeros((tq, KEY_TILE), f32)
        for s in range(n_slabs):
            wa = jnp.concatenate([wb_ref[2 * s]] * lane_chunks, axis=1)
            wo = jnp.concatenate([wb_ref[2 * s + 1]] * lane_chunks, axis=1)
            acc = acc + wa * jnp.maximum(za[s * tq:(s + 1) * tq], 0.0)
            acc = acc + wo * jnp.maximum(zb[s * tq:(s + 1) * tq], 0.0)
        score = jnp.where(j * KEY_TILE + k_off <= t_pos, acc, -jnp.inf)
        bits = lax.bitcast_convert_type(score, jnp.int32)
        key = bits ^ ((bits >> 31) & 0x7FFFFFFF)
        keys_ref[j] = key
        for cc in range(lane_chunks):
            keyst_ref[j, cc * LANES:(cc + 1) * LANES, :] = key[:, cc * LANES:(cc + 1) * LANES].T
        return carry

    sweep_tiles(score_tile)

    def rows_to_lanes_sum(hit):
        return jnp.sum(hit.reshape(KEY_TILE // 8, 8, tq), axis=0)

    def count_ge(cand):
        def body(j, cnt):
            return cnt + rows_to_lanes_sum(jnp.where(keyst_ref[j] >= cand, 1, 0))
        cnt = sweep_tiles(body, jnp.zeros((8, tq), jnp.int32), group=4)
        return jnp.sum(cnt, axis=0, keepdims=True)

    def bisect(i, thr):
        cand = thr + lax.shift_left(jnp.int32(1), 31 - i)
        return jnp.where(count_ge(cand) >= n_sel, cand, thr)

    thr_t = lax.fori_loop(0, 32, bisect, jnp.full((1, tq), _INT_MIN, jnp.int32))

    def per_query_rows(vec_t):
        rows = jnp.broadcast_to(vec_t, (tq, tq)).T
        return jnp.concatenate([rows] * lane_chunks, axis=1)

    thr_wide = per_query_rows(thr_t)

    tied = (count_ge(thr_t) > n_sel) & (thr_t >= _KEY_LOWEST_FINITE)
    any_tied = jnp.max(jnp.where(tied, 1, 0)) > 0

    @pl.when(any_tied)
    def _():
        need = n_sel - count_ge(thr_t + 1)
        row_idx = lax.broadcasted_iota(jnp.int32, (KEY_TILE, tq), 0)

        def tied_before(limit):
            def body(j, cnt):
                hit = jnp.where(j * KEY_TILE + row_idx < limit, 1, 0)
                return cnt + rows_to_lanes_sum(jnp.where(keyst_ref[j] == thr_t, hit, 0))
            cnt = lax.fori_loop(0, n_tiles, body, jnp.zeros((8, tq), jnp.int32))
            return jnp.sum(cnt, axis=0, keepdims=True)

        def bisect_index(i, last):
            cand = last + lax.shift_left(jnp.int32(1), idx_bits - 1 - i)
            return jnp.where(tied_before(cand) < need, cand, last)

        last_t = lax.fori_loop(0, idx_bits, bisect_index, jnp.zeros((1, tq), jnp.int32))
        last_wide = per_query_rows(last_t)

        def demote(j, carry):
            kk = keys_ref[j]
            drop = jnp.where(j * KEY_TILE + k_off > last_wide, thr_wide - 1, kk)
            keys_ref[j] = jnp.where(kk == thr_wide, drop, kk)
            return carry

        lax.fori_loop(0, n_tiles, demote, 0)

    thr_wide = jnp.maximum(thr_wide, _KEY_LOWEST_FINITE)

    q = q_ref[...]
    ones_col = jnp.where(lax.broadcasted_iota(jnp.int32, (KEY_TILE, HEAD_DIM), 1) == 0, 1.0, 0.0
                         ).astype(jnp.bfloat16)
    for g in range(N_KV_HEADS):
        qg = jnp.concatenate([q[:, (g * rep + r) * HEAD_DIM:(g * rep + r + 1) * HEAD_DIM] for r in range(rep)],
                             axis=0)
        m_ref[...] = jnp.full(m_ref.shape, _NEG, f32)
        acc_ref[...] = jnp.zeros(acc_ref.shape, f32)

        def scores(j, carry, qg=qg, g=g):
            bias = jnp.where(keys_ref[j] >= thr_wide, 0.0, _NEG)
            s = jnp.dot(qg, kt_ref[0, g, j], preferred_element_type=f32) + jnp.concatenate([bias] * rep, axis=0)
            s_ref[j] = s
            m_part = m_ref[...]
            for cc in range(lane_chunks):
                m_part = jnp.maximum(m_part, s[:, cc * LANES:(cc + 1) * LANES])
            m_ref[...] = m_part
            return carry

        sweep_tiles(scores, group=4)
        m_ref[...] = jnp.broadcast_to(jnp.max(m_ref[...], axis=-1, keepdims=True), m_ref.shape)

        def values(j, carry, g=g):
            m_row = m_ref[...]
            p = jnp.exp2(s_ref[j] - jnp.concatenate([m_row] * lane_chunks, axis=1)).astype(jnp.bfloat16)
            start = pl.multiple_of(j * KEY_TILE, KEY_TILE)
            vt = jnp.concatenate([v_ref[pl.ds(start, KEY_TILE), g * HEAD_DIM:(g + 1) * HEAD_DIM], ones_col],
                                 axis=1)
            acc_ref[...] += jnp.dot(p, vt, preferred_element_type=f32)
            return carry

        sweep_tiles(values, group=4)
        acc = acc_ref[...]
        o = acc[:, :HEAD_DIM] / acc[:, HEAD_DIM:HEAD_DIM + 1]
        for r in range(rep):
            hd = g * rep + r
            o_ref[:, hd * HEAD_DIM:(hd + 1) * HEAD_DIM] = o[r * tq:(r + 1) * tq].astype(o_ref.dtype)


def _dsa(q, qi, wi, kia, kib, kt, v, B, S):
    tq = min(Q_TILE, S)
    n_sel = min(TOPK_MAX, S // 4)
    nq = S // tq
    n_kt = S // KEY_TILE
    rep = N_HEADS // N_KV_HEADS
    row = lambda w: pl.BlockSpec((tq, w), lambda b, i: (b * nq + i, 0))
    once = pl.Buffered(1)
    kern = functools.partial(_dsa_kernel, tq=tq, n_sel=n_sel, idx_bits=max(1, (S - 1).bit_length()))
    return pl.pallas_call(
        kern,
        grid=(B, nq),
        in_specs=[row(ATTN_WIDTH), row(IDX_HEADS * IDX_DIM), row(LANES),
                  pl.BlockSpec((1, n_kt, LANES, KEY_TILE), lambda b, i: (b, 0, 0, 0), pipeline_mode=once),
                  pl.BlockSpec((1, n_kt, LANES, KEY_TILE), lambda b, i: (b, 0, 0, 0), pipeline_mode=once),
                  pl.BlockSpec((1, N_KV_HEADS, n_kt, HEAD_DIM, KEY_TILE), lambda b, i: (b, 0, 0, 0, 0),
                               pipeline_mode=once),
                  pl.BlockSpec((S, KV_WIDTH), lambda b, i: (b, 0), pipeline_mode=once)],
        out_specs=row(ATTN_WIDTH),
        out_shape=jax.ShapeDtypeStruct((B * S, ATTN_WIDTH), jnp.bfloat16),
        scratch_shapes=[pltpu.VMEM((n_kt, tq, KEY_TILE), jnp.int32),
                        pltpu.VMEM((n_kt, KEY_TILE, tq), jnp.int32),
                        pltpu.VMEM((IDX_HEADS, tq, LANES), jnp.float32),
                        pltpu.VMEM((rep * tq, LANES), jnp.float32),
                        pltpu.VMEM((rep * tq, 2 * HEAD_DIM), jnp.float32),
                        pltpu.VMEM((n_kt, rep * tq, KEY_TILE), jnp.float32)],
        compiler_params=_cparams(("arbitrary", "arbitrary")),
        name="dsa",
    )(q, qi, wi, kia, kib, kt, v)


_META_EID, _META_W, _META_RANK = 0, 2, 4


def _mix_kernel(attn_ref, p_ref, halo_ref, x_ref, ada_ref, wpool_ref, pscale_ref, wout_ref, g2_ref, wr_ref,
                x1_ref, h2_ref, meta_ref, counts_ref, pbuf_ref, run_ref, wrs_ref, *, tm, tiles_per_seq):
    i = pl.program_id(0)
    f32 = jnp.float32
    bf = jnp.bfloat16
    seq_tile = i % tiles_per_seq

    @pl.when(i == 0)
    def _():
        run_ref[...] = jnp.zeros(run_ref.shape, f32)
        wr = wr_ref[...]
        wr_hi = wr.astype(bf)
        wrs_ref[0] = wr_hi
        wrs_ref[1] = (wr - wr_hi.astype(f32)).astype(bf)

    p = p_ref[...]
    pbuf_ref[0:POOL_HALO, :] = jnp.where(seq_tile == 0, 0.0, halo_ref[...])
    pbuf_ref[POOL_HALO:POOL_HALO + tm, :] = p
    t_pos = seq_tile * tm + lax.broadcasted_iota(jnp.int32, (tm, 1), 0)
    pieces = [attn_ref[...]]
    for g, win in enumerate(POOL_WINDOWS):
        cs = slice(g * POOL_GROUP_DIM, (g + 1) * POOL_GROUP_DIM)
        tot = p[:, cs]
        for back in range(1, win):
            tot = tot + pbuf_ref[POOL_HALO - back:POOL_HALO - back + tm, cs]
        count = jnp.minimum(t_pos + 1, win).astype(f32)
        mixed = tot / count - p[:, cs]
        y = jnp.dot(mixed.astype(bf), wpool_ref[g], preferred_element_type=f32) * pscale_ref[:, cs]
        pieces.append(y.astype(bf))
    mix = jnp.dot(jnp.concatenate(pieces, axis=1), wout_ref[...], preferred_element_type=f32)
    x1 = x_ref[...] + ada_ref[0, 2:3, :] * mix
    x1_ref[...] = x1

    ms = jnp.mean(x1 * x1, axis=-1, keepdims=True)
    h2 = (x1 * lax.rsqrt(ms + NORM_EPS) * g2_ref[...]) * (1.0 + ada_ref[0, 4:5, :]) + ada_ref[0, 3:4, :]
    h2_ref[...] = h2

    h2_hi = h2.astype(bf)
    h2_lo = (h2 - h2_hi.astype(f32)).astype(bf)
    logits = (jnp.dot(h2_hi, wrs_ref[0], preferred_element_type=f32)
              + jnp.dot(h2_lo, wrs_ref[0], preferred_element_type=f32)
              + jnp.dot(h2_hi, wrs_ref[1], preferred_element_type=f32))
    lane = lax.broadcasted_iota(jnp.int32, (tm, LANES), 1)
    big = jnp.int32(LANES)
    rmax = lambda a: jnp.max(a, axis=-1, keepdims=True)
    rmin = lambda a: jnp.min(a, axis=-1, keepdims=True)
    rsum = lambda a: jnp.sum(a, axis=-1, keepdims=True)
    is_grp = lane < N_EXPERT_GROUPS
    m_g = rmax(jnp.where(is_grp, logits, -jnp.inf))
    p_g = 1.0 / rsum(jnp.where(is_grp, jnp.exp(logits - m_g), 0.0))
    g_sel = rmin(jnp.where(is_grp & (logits == m_g), lane, big))
    lo = N_EXPERT_GROUPS + EXPERTS_PER_GROUP * g_sel
    in_grp = (lane >= lo) & (lane < lo + EXPERTS_PER_GROUP)
    le = jnp.where(in_grp, logits, -jnp.inf)
    m_1 = rmax(le)
    i_1 = rmin(jnp.where(le == m_1, lane, big))
    le2 = jnp.where(lane == i_1, -jnp.inf, le)
    m_2 = rmax(le2)
    i_2 = rmin(jnp.where(le2 == m_2, lane, big))
    e_2 = jnp.exp(m_2 - m_1)
    w_1 = p_g / (1.0 + e_2)
    w_2 = p_g * e_2 / (1.0 + e_2)
    eid_1 = i_1 - N_EXPERT_GROUPS
    eid_2 = i_2 - N_EXPERT_GROUPS

    oh1 = lane == eid_1
    oh2 = lane == eid_2
    onehot = jnp.where(oh1, 1.0, 0.0) + jnp.where(oh2, 1.0, 0.0)
    r_i = lax.broadcasted_iota(jnp.int32, (tm, tm), 0)
    c_i = lax.broadcasted_iota(jnp.int32, (tm, tm), 1)
    earlier = jnp.where(c_i < r_i, 1.0, 0.0).astype(bf)
    before = jnp.dot(earlier, onehot.astype(bf), preferred_element_type=f32) + run_ref[...]
    rank_1 = rsum(jnp.where(oh1, before, 0.0))
    rank_2 = rsum(jnp.where(oh2, before, 0.0))
    run_ref[...] = run_ref[...] + jnp.sum(onehot, axis=0, keepdims=True)
    counts_ref[...] = run_ref[...]

    meta = jnp.zeros((tm, LANES), f32)
    for off, val in ((_META_EID, eid_1.astype(f32)), (_META_EID + 1, eid_2.astype(f32)),
                     (_META_W, w_1), (_META_W + 1, w_2), (_META_RANK, rank_1), (_META_RANK + 1, rank_2)):
        meta = jnp.where(lane == off, val, meta)
    meta_ref[...] = meta


def _mix(attn, p, x2d, ada3, w_pool, pool_scale, w_out, norm2_g, w_grp, w_exp, B, S):
    T, D = x2d.shape
    tm = min(256, S)
    tiles_per_seq = S // tm
    halo_blocks = tm // POOL_HALO
    bf = jnp.bfloat16
    w_router = jnp.concatenate(
        [w_grp, w_exp, jnp.zeros((D, LANES - N_EXPERT_GROUPS - N_EXPERTS), w_grp.dtype)], axis=1)
    row = lambda w: pl.BlockSpec((tm, w), lambda i: (i, 0))
    const = lambda shape: pl.BlockSpec(shape, lambda i: (0,) * len(shape))
    once = lambda shape: pl.BlockSpec(shape, lambda i: (0,) * len(shape), pipeline_mode=pl.Buffered(1))
    kern = functools.partial(_mix_kernel, tm=tm, tiles_per_seq=tiles_per_seq)
    return pl.pallas_call(
        kern,
        grid=(T // tm,),
        in_specs=[row(ATTN_WIDTH), row(POOL_WIDTH),
                  pl.BlockSpec((POOL_HALO, POOL_WIDTH), lambda i: (jnp.maximum(i * halo_blocks - 1, 0), 0)),
                  row(D),
                  pl.BlockSpec((1, 6, D), lambda i: (i // tiles_per_seq, 0, 0)),
                  once((POOL_GROUPS, POOL_GROUP_DIM, POOL_GROUP_DIM)), const((1, POOL_WIDTH)),
                  once((ATTN_WIDTH + POOL_WIDTH, D)), const((1, D)), once((D, LANES))],
        out_specs=[row(D), row(D), row(LANES), const((1, LANES))],
        out_shape=[jax.ShapeDtypeStruct((T, D), jnp.float32),
                   jax.ShapeDtypeStruct((T, D), jnp.float32),
                   jax.ShapeDtypeStruct((T, LANES), jnp.float32),
                   jax.ShapeDtypeStruct((1, LANES), jnp.float32)],
        scratch_shapes=[pltpu.VMEM((POOL_HALO + tm, POOL_WIDTH), jnp.float32),
                        pltpu.VMEM((1, LANES), jnp.float32),
                        pltpu.VMEM((2, D, LANES), bf)],
        compiler_params=_cparams(("arbitrary",)),
        name="mix",
    )(attn, p, p, x2d, ada3, w_pool.astype(bf), pool_scale.reshape(1, POOL_WIDTH), w_out.astype(bf),
      norm2_g.reshape(1, D), w_router)


def _dispatch_kernel(pad_start_ref, pad_len_ref, n_used_ref, pos_ref, h2_ref, xs_ref, zeros_ref, sem, *, rt, n_tiles):
    i = pl.program_id(0)

    @pl.when(i == 0)
    def _():
        zeros_ref[...] = jnp.zeros(zeros_ref.shape, zeros_ref.dtype)

        def zero_row(r):
            return pltpu.make_async_copy(zeros_ref.at[pl.ds(0, 1)], xs_ref.at[pl.ds(r, 1)], sem)

        def zero_tile(t):
            start = pl.multiple_of(t * EXPERT_TILE, EXPERT_TILE)
            return pltpu.make_async_copy(zeros_ref, xs_ref.at[pl.ds(start, EXPERT_TILE)], sem)

        def fill(e, carry):
            start = pad_start_ref[e]
            n = pad_len_ref[e]
            lax.fori_loop(0, n, lambda r, c: (zero_row(start + r).start(), c)[1], 0)
            lax.fori_loop(0, n, lambda r, c: (zero_row(0).wait(), c)[1], 0)
            return carry

        lax.fori_loop(0, N_EXPERTS, fill, 0)
        first_unused = n_used_ref[0]
        lax.fori_loop(first_unused, n_tiles, lambda t, c: (zero_tile(t).start(), c)[1], 0)
        lax.fori_loop(first_unused, n_tiles, lambda t, c: (zero_tile(0).wait(), c)[1], 0)

    def row_copy(t, slot):
        return pltpu.make_async_copy(h2_ref.at[pl.ds(t, 1)], xs_ref.at[pl.ds(pos_ref[0, 0, 2 * t + slot], 1)], sem)

    def issue(t, carry):
        row_copy(t, 0).start(priority=0)
        row_copy(t, 1).start(priority=1)
        return carry

    def drain(t, carry):
        for _ in range(2):
            pltpu.make_async_copy(h2_ref.at[pl.ds(0, 1)], xs_ref.at[pl.ds(0, 1)], sem).wait()
        return carry

    lax.fori_loop(0, rt, issue, 0)
    lax.fori_loop(0, rt, drain, 0)


def _dispatch(h2, pos, pad_start, pad_len, n_used, n_rows):
    T, D = h2.shape
    rt = min(ROW_TILE, T)
    kern = functools.partial(_dispatch_kernel, rt=rt, n_tiles=n_rows // EXPERT_TILE)
    return pl.pallas_call(
        kern,
        grid_spec=pltpu.PrefetchScalarGridSpec(
            num_scalar_prefetch=3,
            grid=(T // rt,),
            in_specs=[pl.BlockSpec((1, 1, 2 * rt), lambda i, *_: (i, 0, 0), memory_space=pltpu.SMEM),
                      pl.BlockSpec((rt, D), lambda i, *_: (i, 0))],
            out_specs=pl.BlockSpec(memory_space=pl.ANY),
            scratch_shapes=[pltpu.VMEM((EXPERT_TILE, D), h2.dtype), pltpu.SemaphoreType.DMA(())]),
        out_shape=jax.ShapeDtypeStruct((n_rows, D), h2.dtype),
        compiler_params=_cparams(("arbitrary",)),
        name="dispatch",
    )(pad_start, pad_len, n_used, pos.reshape(T // rt, 1, 2 * rt), h2)


def _experts_kernel(tile_expert_ref, n_used_ref, xs_ref, w1_ref, w3_ref, w2_ref, ys_ref, w1b_ref, w3b_ref, w2b_ref):
    i = pl.program_id(0)
    bf = jnp.bfloat16

    @pl.when((i == 0) | (tile_expert_ref[i] != tile_expert_ref[jnp.maximum(i - 1, 0)]))
    def _():
        w1b_ref[...] = w1_ref[0].astype(bf)
        w3b_ref[...] = w3_ref[0].astype(bf)
        w2b_ref[...] = w2_ref[0].astype(bf)

    @pl.when(i < n_used_ref[0])
    def _():
        xb = xs_ref[...].astype(bf)
        a1 = jnp.dot(xb, w1b_ref[...], preferred_element_type=jnp.float32)
        a3 = jnp.dot(xb, w3b_ref[...], preferred_element_type=jnp.float32)
        act = (a1 * (1.0 / (1.0 + jnp.exp(-a1))) * a3).astype(bf)
        ys_ref[...] = jnp.dot(act, w2b_ref[...], preferred_element_type=jnp.float32)

    @pl.when(i >= n_used_ref[0])
    def _():
        ys_ref[...] = jnp.zeros(ys_ref.shape, ys_ref.dtype)


def _experts(xs, w1, w3, w2, tile_expert, n_used):
    n_rows, D = xs.shape
    n_tiles = n_rows // EXPERT_TILE
    bf = jnp.bfloat16
    return pl.pallas_call(
        _experts_kernel,
        grid_spec=pltpu.PrefetchScalarGridSpec(
            num_scalar_prefetch=2,
            grid=(n_tiles,),
            in_specs=[pl.BlockSpec((EXPERT_TILE, D), lambda i, te, nu: (jnp.minimum(i, nu[0] - 1), 0)),
                      pl.BlockSpec((1, D, D_EXPERT), lambda i, te, nu: (te[i], 0, 0)),
                      pl.BlockSpec((1, D, D_EXPERT), lambda i, te, nu: (te[i], 0, 0)),
                      pl.BlockSpec((1, D_EXPERT, D), lambda i, te, nu: (te[i], 0, 0))],
            out_specs=pl.BlockSpec((EXPERT_TILE, D), lambda i, te, nu: (i, 0)),
            scratch_shapes=[pltpu.VMEM((D, D_EXPERT), bf), pltpu.VMEM((D, D_EXPERT), bf),
                            pltpu.VMEM((D_EXPERT, D), bf)]),
        out_shape=jax.ShapeDtypeStruct((n_rows, D), jnp.float32),
        compiler_params=_cparams(("arbitrary",)),
        name="experts",
    )(tile_expert, n_used, xs, w1, w3, w2)


def _combine_kernel(pos_ref, ys_ref, x1_ref, meta_ref, ada_ref, o_ref, rows_ref, sem, *, rt):
    def row_copy(t, slot):
        return pltpu.make_async_copy(ys_ref.at[pl.ds(pos_ref[0, 0, 2 * t + slot], 1)],
                                     rows_ref.at[slot, pl.ds(t, 1)], sem)

    def issue(t, carry):
        row_copy(t, 0).start(priority=0)
        row_copy(t, 1).start(priority=1)
        return carry

    def drain(t, carry):
        for slot in range(2):
            pltpu.make_async_copy(ys_ref.at[pl.ds(0, 1)], rows_ref.at[slot, pl.ds(0, 1)], sem).wait()
        return carry

    lax.fori_loop(0, rt, issue, 0)
    lax.fori_loop(0, rt, drain, 0)
    meta = meta_ref[...]
    y = meta[:, _META_W:_META_W + 1] * rows_ref[0] + meta[:, _META_W + 1:_META_W + 2] * rows_ref[1]
    o_ref[...] = x1_ref[...] + ada_ref[0, 5:6, :] * y


def _combine(ys, pos, x1, meta, ada3, S):
    T, D = x1.shape
    rt = min(ROW_TILE, S)
    tiles_per_seq = S // rt
    kern = functools.partial(_combine_kernel, rt=rt)
    return pl.pallas_call(
        kern,
        grid=(T // rt,),
        in_specs=[pl.BlockSpec((1, 1, 2 * rt), lambda i: (i, 0, 0), memory_space=pltpu.SMEM),
                  pl.BlockSpec(memory_space=pl.ANY),
                  pl.BlockSpec((rt, D), lambda i: (i, 0)),
                  pl.BlockSpec((rt, LANES), lambda i: (i, 0)),
                  pl.BlockSpec((1, 6, D), lambda i: (i // tiles_per_seq, 0, 0))],
        out_specs=pl.BlockSpec((rt, D), lambda i: (i, 0)),
        out_shape=jax.ShapeDtypeStruct((T, D), jnp.float32),
        scratch_shapes=[pltpu.VMEM((2, rt, D), jnp.float32), pltpu.SemaphoreType.DMA(())],
        compiler_params=_cparams(("arbitrary",)),
        name="combine",
    )(pos.reshape(T // rt, 1, 2 * rt), ys, x1, meta, ada3)


def _routing_plan(meta, counts, T):
    i32 = jnp.int32
    eid = meta[:, _META_EID:_META_EID + 2].reshape(-1).astype(i32)
    rank = meta[:, _META_RANK:_META_RANK + 2].reshape(-1).astype(i32)
    cnt = counts[0, :N_EXPERTS].astype(i32)
    padded = (cnt + EXPERT_TILE - 1) // EXPERT_TILE * EXPERT_TILE
    ends = jnp.cumsum(padded)
    starts = ends - padded
    pos = starts[eid] + rank
    n_rows = 2 * T + N_EXPERTS * EXPERT_TILE
    n_tiles = n_rows // EXPERT_TILE
    n_used = (ends[-1] // EXPERT_TILE).astype(i32)
    tile_start = jnp.minimum(jnp.arange(n_tiles, dtype=i32), n_used - 1) * EXPERT_TILE
    tile_expert = jnp.sum((ends[None, :] <= tile_start[:, None]).astype(i32), axis=1)
    return pos, starts + cnt, padded - cnt, tile_expert, n_used.reshape(1), n_rows


def kernel(x, c, w_ada, b_ada, norm1_g, w_in, q_norm_g, k_norm_g, w_pool, pool_scale,
           w_out, norm2_g, w_grp, w_exp, w1, w3, w2):
    B, S, D = x.shape
    T = B * S
    x2d = x.reshape(T, D)
    for l in range(w_ada.shape[0]):
        ada3 = _ada(c, w_ada[l], b_ada[l]).reshape(B, 6, D)
        q, kt, v, qi, kia, kib, wi, p = _inproj(x2d, ada3, norm1_g[l], w_in[l], q_norm_g[l], k_norm_g[l], B, S)
        attn = _dsa(q, qi, wi, kia, kib, kt, v, B, S)
        x1, h2, meta, counts = _mix(attn, p, x2d, ada3, w_pool[l], pool_scale[l], w_out[l], norm2_g[l],
                                    w_grp[l], w_exp[l], B, S)
        pos, pad_start, pad_len, tile_expert, n_used, n_rows = _routing_plan(meta, counts, T)
        xs = _dispatch(h2, pos, pad_start, pad_len, n_used, n_rows)
        ys = _experts(xs, w1[l], w3[l], w2[l], tile_expert, n_used)
        x2d = _combine(ys, pos, x1, meta, ada3, S)
    return x2d.reshape(B, S, D)
```

```python
import functools

import numpy as np
import jax
import jax.numpy as jnp
from jax import lax
from jax.experimental import pallas as pl
from jax.experimental.pallas import tpu as pltpu

N_HEADS = 8
N_KV_HEADS = 2
HEAD_DIM = 128
ATTN_WIDTH = N_HEADS * HEAD_DIM
KV_WIDTH = N_KV_HEADS * HEAD_DIM
ROPE_THETA = 500000.0
ROPE_FRACTION = 4
IDX_HEADS = 16
IDX_DIM = 64
TOPK_MAX = 256
POOL_GROUPS = 4
POOL_WINDOWS = (2, 4, 8, 16)
POOL_GROUP_DIM = 256
POOL_WIDTH = POOL_GROUPS * POOL_GROUP_DIM
N_EXPERT_GROUPS = 4
EXPERTS_PER_GROUP = 8
N_EXPERTS = N_EXPERT_GROUPS * EXPERTS_PER_GROUP
D_EXPERT = 512
NORM_EPS = 1e-6

LANES = 128
VMEM_LIMIT_BYTES = 56 * 1024 * 1024

KEY_TILE = 512
Q_TILE = 128
POOL_HALO = 16
EXPERT_TILE = 256
ROW_TILE = 512

_C_Q = 0
_C_K = _C_Q + ATTN_WIDTH
_C_V = _C_K + KV_WIDTH
_C_QI = _C_V + KV_WIDTH
_C_KI = _C_QI + IDX_HEADS * IDX_DIM
_C_WI = _C_KI + LANES
_C_P = _C_WI + LANES
_C_END = _C_P + POOL_WIDTH

_NEG = -1e30
_LOG2_E = 1.4426950408889634
_INT_MIN = -2 ** 31
_KEY_LOWEST_FINITE = -2 ** 31 + 0x00800000
_BISECT_FIXED_PASSES = 22


def _cparams(semantics):
    return pltpu.CompilerParams(dimension_semantics=semantics, vmem_limit_bytes=VMEM_LIMIT_BYTES)


def _ada_kernel(c_ref, w_ref, b_ref, o_ref):
    c = c_ref[...]
    s = c * (1.0 / (1.0 + jnp.exp(-c)))
    o_ref[...] = jnp.dot(s, w_ref[...], preferred_element_type=jnp.float32,
                         precision=lax.Precision.HIGHEST) + b_ref[...]


def _ada(c, w_ada, b_ada):
    B, D = c.shape
    N = w_ada.shape[1]
    tn = 1024
    rows = 8
    c_pad = jnp.zeros((rows, D), jnp.float32).at[:B].set(c)
    out = pl.pallas_call(
        _ada_kernel,
        grid=(N // tn,),
        in_specs=[pl.BlockSpec((rows, D), lambda j: (0, 0)),
                  pl.BlockSpec((D, tn), lambda j: (0, j)),
                  pl.BlockSpec((1, tn), lambda j: (0, j))],
        out_specs=pl.BlockSpec((rows, tn), lambda j: (0, j)),
        out_shape=jax.ShapeDtypeStruct((rows, N), jnp.float32),
        compiler_params=_cparams(("arbitrary",)),
        name="ada",
    )(c_pad, w_ada, b_ada.reshape(1, N))
    return out[:B]


def _rope_tables(S, head_dim):
    rd = head_dim // ROPE_FRACTION
    half = rd // 2
    pos = jnp.arange(S, dtype=jnp.float32)
    inv = jnp.float32(ROPE_THETA) ** (-(jnp.arange(half, dtype=jnp.float32) * 2.0) / rd)
    ang = pos[:, None] * inv[None, :]
    cos, sin = jnp.cos(ang), jnp.sin(ang)
    lane = np.arange(LANES) % head_dim
    fidx = np.where(lane < half, lane, lane - half) % half
    in_lo = jnp.asarray(lane < half)[None, :]
    in_hi = jnp.asarray((lane >= half) & (lane < rd))[None, :]
    cos_l, sin_l = cos[:, fidx], sin[:, fidx]
    c_tab = jnp.where(in_lo | in_hi, cos_l, 1.0)
    s1_tab = jnp.where(in_lo, -sin_l, 0.0)
    s2_tab = jnp.where(in_hi, sin_l, 0.0)
    return c_tab, s1_tab, s2_tab, half


def _rope(x, c_tab, s1_tab, s2_tab, half):
    return (x * c_tab + pltpu.roll(x, LANES - half, 1) * s1_tab + pltpu.roll(x, half, 1) * s2_tab)


def _inproj_kernel(x_ref, ada_ref, g1_ref, w_ref, qg_ref, kg_ref,
                   cq_ref, s1q_ref, s2q_ref, ci_ref, s1i_ref, s2i_ref,
                   q_ref, kt_ref, v_ref, qi_ref, kia_ref, kib_ref, wi_ref, p_ref,
                   *, half_qk, half_idx, n_chunks):
    x = x_ref[...]
    ms = jnp.mean(x * x, axis=-1, keepdims=True)
    xn = x * lax.rsqrt(ms + NORM_EPS) * g1_ref[...]
    h = (xn * (1.0 + ada_ref[0, 1:2, :]) + ada_ref[0, 0:1, :]).astype(jnp.bfloat16)

    def proj(lo, hi):
        return jnp.dot(h, w_ref[:, lo:hi], preferred_element_type=jnp.float32)

    cq, s1q, s2q = cq_ref[...], s1q_ref[...], s2q_ref[...]
    ci, s1i, s2i = ci_ref[...], s1i_ref[...], s2i_ref[...]

    def qk_head(slab, gain):
        m = jnp.mean(slab * slab, axis=-1, keepdims=True)
        y = slab * lax.rsqrt(m + NORM_EPS) * gain
        return _rope(y, cq, s1q, s2q, half_qk)

    attn_scale = HEAD_DIM ** -0.5 * _LOG2_E
    q = proj(_C_Q, _C_K)
    for hd in range(N_HEADS):
        sl = slice(hd * HEAD_DIM, (hd + 1) * HEAD_DIM)
        q_ref[:, sl] = (qk_head(q[:, sl], qg_ref[...]) * attn_scale).astype(jnp.bfloat16)

    k = proj(_C_K, _C_V)
    for g in range(N_KV_HEADS):
        kt = qk_head(k[:, g * HEAD_DIM:(g + 1) * HEAD_DIM], kg_ref[...]).T.astype(jnp.bfloat16)
        for c in range(n_chunks):
            kt_ref[0, g, c] = kt[:, c * KEY_TILE:(c + 1) * KEY_TILE]

    v_ref[...] = proj(_C_V, _C_QI).astype(jnp.bfloat16)

    qi = proj(_C_QI, _C_KI)
    for j in range(IDX_HEADS * IDX_DIM // LANES):
        sl = slice(j * LANES, (j + 1) * LANES)
        qi_ref[:, sl] = _rope(qi[:, sl], ci, s1i, s2i, half_idx).astype(jnp.bfloat16)

    ki = _rope(proj(_C_KI, _C_WI), ci, s1i, s2i, half_idx).T
    ki_swapped = jnp.concatenate([ki[IDX_DIM:], ki[:IDX_DIM]], axis=0)
    for c in range(n_chunks):
        kia_ref[0, c] = ki[:, c * KEY_TILE:(c + 1) * KEY_TILE].astype(jnp.bfloat16)
        kib_ref[0, c] = ki_swapped[:, c * KEY_TILE:(c + 1) * KEY_TILE].astype(jnp.bfloat16)

    idx_scale = (IDX_DIM ** -0.5) * (IDX_HEADS ** -0.5)
    wi_ref[...] = proj(_C_WI, _C_P) * idx_scale
    p_ref[...] = proj(_C_P, _C_END)


def _inproj(x2d, ada3, norm1_g, w_in, q_norm_g, k_norm_g, B, S):
    T, D = x2d.shape
    tm = min(512, S)
    n_chunks = tm // KEY_TILE
    tiles_per_seq = S // tm
    bf = jnp.bfloat16

    offs = np.cumsum((ATTN_WIDTH, KV_WIDTH, KV_WIDTH, IDX_HEADS * IDX_DIM, IDX_DIM, IDX_HEADS))
    o_q, o_k, o_v, o_qi, o_ki, o_wi = [int(o) for o in offs]
    zeros = lambda n: jnp.zeros((D, n), w_in.dtype)
    w_pad = jnp.concatenate([
        w_in[:, :o_qi],
        w_in[:, o_qi:o_ki], zeros(LANES - IDX_DIM),
        w_in[:, o_ki:o_wi], zeros(LANES - IDX_HEADS),
        w_in[:, o_wi:],
    ], axis=1).astype(bf)
    assert w_pad.shape[1] == _C_END

    cq, s1q, s2q, half_qk = _rope_tables(S, HEAD_DIM)
    ci, s1i, s2i, half_idx = _rope_tables(S, IDX_DIM)

    row = lambda w: pl.BlockSpec((tm, w), lambda i: (i, 0))
    tab = pl.BlockSpec((tm, LANES), lambda i: (i % tiles_per_seq, 0))
    const = lambda shape: pl.BlockSpec(shape, lambda i: (0,) * len(shape))
    kern = functools.partial(_inproj_kernel, half_qk=half_qk, half_idx=half_idx, n_chunks=n_chunks)
    return pl.pallas_call(
        kern,
        grid=(T // tm,),
        in_specs=[row(D),
                  pl.BlockSpec((1, 6, D), lambda i: (i // tiles_per_seq, 0, 0)),
                  const((1, D)),
                  pl.BlockSpec((D, _C_END), lambda i: (0, 0), pipeline_mode=pl.Buffered(1)),
                  const((1, HEAD_DIM)), const((1, HEAD_DIM)),
                  tab, tab, tab, tab, tab, tab],
        out_specs=[row(ATTN_WIDTH),
                   pl.BlockSpec((1, N_KV_HEADS, n_chunks, HEAD_DIM, KEY_TILE),
                                lambda i: (i // tiles_per_seq, 0, i % tiles_per_seq, 0, 0)),
                   row(KV_WIDTH),
                   row(IDX_HEADS * IDX_DIM),
                   pl.BlockSpec((1, n_chunks, LANES, KEY_TILE),
                                lambda i: (i // tiles_per_seq, i % tiles_per_seq, 0, 0)),
                   pl.BlockSpec((1, n_chunks, LANES, KEY_TILE),
                                lambda i: (i // tiles_per_seq, i % tiles_per_seq, 0, 0)),
                   row(LANES),
                   row(POOL_WIDTH)],
        out_shape=[jax.ShapeDtypeStruct((T, ATTN_WIDTH), bf),
                   jax.ShapeDtypeStruct((B, N_KV_HEADS, S // KEY_TILE, HEAD_DIM, KEY_TILE), bf),
                   jax.ShapeDtypeStruct((T, KV_WIDTH), bf),
                   jax.ShapeDtypeStruct((T, IDX_HEADS * IDX_DIM), bf),
                   jax.ShapeDtypeStruct((B, S // KEY_TILE, LANES, KEY_TILE), bf),
                   jax.ShapeDtypeStruct((B, S // KEY_TILE, LANES, KEY_TILE), bf),
                   jax.ShapeDtypeStruct((T, LANES), jnp.float32),
                   jax.ShapeDtypeStruct((T, POOL_WIDTH), jnp.float32)],
        compiler_params=_cparams(("arbitrary",)),
        name="inproj",
    )(x2d, ada3, norm1_g.reshape(1, D), w_pad, q_norm_g.reshape(1, HEAD_DIM), k_norm_g.reshape(1, HEAD_DIM),
      cq, s1q, s2q, ci, s1i, s2i)


def _dsa_kernel(q_ref, qi_ref, wi_ref, kia_ref, kib_ref, kt_ref, v_ref, o_ref,
                keys_ref, keyst_ref, wb_ref, m_ref, acc_ref, s_ref, *, tq, n_sel, idx_bits):
    qb = pl.program_id(1)
    n_tiles = (qb * tq + tq + KEY_TILE - 1) // KEY_TILE
    lane_chunks = KEY_TILE // LANES
    rep = N_HEADS // N_KV_HEADS
    f32 = jnp.float32

    qi = qi_ref[...]
    n_slabs = IDX_HEADS * IDX_DIM // LANES
    lhs = jnp.concatenate([qi[:, j * LANES:(j + 1) * LANES] for j in range(n_slabs)], axis=0)
    w = wi_ref[...]
    for hd in range(IDX_HEADS):
        wb_ref[hd] = jnp.broadcast_to(w[:, hd:hd + 1], (tq, LANES))
    t_pos = qb * tq + lax.broadcasted_iota(jnp.int32, (tq, KEY_TILE), 0)
    k_off = lax.broadcasted_iota(jnp.int32, (tq, KEY_TILE), 1)

    def sweep_tiles(body, init=0, group=2):
        shift = group.bit_length() - 1
        n_groups = lax.shift_right_logical(n_tiles, shift)

        def grouped(i, carry):
            for u in range(group):
                carry = body(group * i + u, carry)
            return carry

        carry = lax.fori_loop(0, n_groups, grouped, init)
        return lax.fori_loop(group * n_groups, n_tiles, body, carry)

    def score_tile(j, carry):
        za = jnp.dot(lhs, kia_ref[0, j], preferred_element_type=f32)
        zb = jnp.dot(lhs, kib_ref[0, j], preferred_element_type=f32)
        acc = jnp.zeros((tq, KEY_TILE), f32)
        for s in range(n_slabs):
            wa = jnp.concatenate([wb_ref[2 * s]] * lane_chunks, axis=1)
            wo = jnp.concatenate([wb_ref[2 * s + 1]] * lane_chunks, axis=1)
            acc = acc + wa * jnp.maximum(za[s * tq:(s + 1) * tq], 0.0)
            acc = acc + wo * jnp.maximum(zb[s * tq:(s + 1) * tq], 0.0)
        score = jnp.where(j * KEY_TILE + k_off <= t_pos, acc, -jnp.inf)
        bits = lax.bitcast_convert_type(score, jnp.int32)
        key = bits ^ ((bits >> 31) & 0x7FFFFFFF)
        keys_ref[j] = key
        for cc in range(lane_chunks):
            keyst_ref[j, cc * LANES:(cc + 1) * LANES, :] = key[:, cc * LANES:(cc + 1) * LANES].T
        return carry

    sweep_tiles(score_tile)

    def rows_to_lanes_sum(hit):
        return jnp.sum(hit.reshape(KEY_TILE // 8, 8, tq), axis=0)

    def count_ge(cand):
        def body(j, cnt):
            return cnt + rows_to_lanes_sum(jnp.where(keyst_ref[j] >= cand, 1, 0))
        cnt = sweep_tiles(body, jnp.zeros((8, tq), jnp.int32), group=4)
        return jnp.sum(cnt, axis=0, keepdims=True)

    def bisect(i, state):
        thr, at_thr = state
        cand = thr + lax.shift_left(jnp.int32(1), 31 - i)
        cnt = count_ge(cand)
        take = cnt >= n_sel
        return jnp.where(take, cand, thr), jnp.where(take, cnt, at_thr)

    state = (jnp.full((1, tq), _INT_MIN, jnp.int32), jnp.full((1, tq), 2 ** 31 - 1, jnp.int32))
    state = lax.fori_loop(0, _BISECT_FIXED_PASSES, bisect, state)

    def unsettled(carry):
        i, _, at_thr = carry
        return (i < 32) & (jnp.max(jnp.where(at_thr == n_sel, 0, 1)) > 0)

    def two_more_bits(carry):
        i, thr, at_thr = carry
        thr, at_thr = bisect(i + 1, bisect(i, (thr, at_thr)))
        return i + 2, thr, at_thr

    _, thr_t, at_thr_t = lax.while_loop(unsettled, two_more_bits, (jnp.int32(_BISECT_FIXED_PASSES),) + state)

    def per_query_rows(vec_t):
        rows = jnp.broadcast_to(vec_t, (tq, tq)).T
        return jnp.concatenate([rows] * lane_chunks, axis=1)

    thr_wide = per_query_rows(thr_t)

    tied = (at_thr_t > n_sel) & (thr_t >= _KEY_LOWEST_FINITE)
    any_tied = jnp.max(jnp.where(tied, 1, 0)) > 0

    @pl.when(any_tied)
    def _():
        need = n_sel - count_ge(thr_t + 1)
        row_idx = lax.broadcasted_iota(jnp.int32, (KEY_TILE, tq), 0)

        def tied_before(limit):
            def body(j, cnt):
                hit = jnp.where(j * KEY_TILE + row_idx < limit, 1, 0)
                return cnt + rows_to_lanes_sum(jnp.where(keyst_ref[j] == thr_t, hit, 0))
            cnt = lax.fori_loop(0, n_tiles, body, jnp.zeros((8, tq), jnp.int32))
            return jnp.sum(cnt, axis=0, keepdims=True)

        def bisect_index(i, last):
            cand = last + lax.shift_left(jnp.int32(1), idx_bits - 1 - i)
            return jnp.where(tied_before(cand) < need, cand, last)

        last_t = lax.fori_loop(0, idx_bits, bisect_index, jnp.zeros((1, tq), jnp.int32))
        last_wide = per_query_rows(last_t)

        def demote(j, carry):
            kk = keys_ref[j]
            drop = jnp.where(j * KEY_TILE + k_off > last_wide, thr_wide - 1, kk)
            keys_ref[j] = jnp.where(kk == thr_wide, drop, kk)
            return carry

        lax.fori_loop(0, n_tiles, demote, 0)

    thr_wide = jnp.maximum(thr_wide, _KEY_LOWEST_FINITE)

    q = q_ref[...]
    ones_col = jnp.ones((KEY_TILE, HEAD_DIM), jnp.bfloat16)
    for g in range(N_KV_HEADS):
        qg = jnp.concatenate([q[:, (g * rep + r) * HEAD_DIM:(g * rep + r + 1) * HEAD_DIM] for r in range(rep)],
                             axis=0)
        m_ref[...] = jnp.full(m_ref.shape, _NEG, f32)
        acc_ref[...] = jnp.zeros(acc_ref.shape, f32)

        def scores(j, carry, qg=qg, g=g):
            bias = jnp.where(keys_ref[j] >= thr_wide, 0.0, _NEG)
            s = jnp.dot(qg, kt_ref[0, g, j], preferred_element_type=f32) + jnp.concatenate([bias] * rep, axis=0)
            s_ref[j] = s
            m_part = m_ref[...]
            for cc in range(lane_chunks):
                m_part = jnp.maximum(m_part, s[:, cc * LANES:(cc + 1) * LANES])
            m_ref[...] = m_part
            return carry

        sweep_tiles(scores, group=4)
        m_ref[...] = jnp.broadcast_to(jnp.max(m_ref[...], axis=-1, keepdims=True), m_ref.shape)

        def values(j, carry, g=g):
            m_row = m_ref[...]
            p = jnp.exp2(s_ref[j] - jnp.concatenate([m_row] * lane_chunks, axis=1)).astype(jnp.bfloat16)
            start = pl.multiple_of(j * KEY_TILE, KEY_TILE)
            vt = jnp.concatenate([v_ref[pl.ds(start, KEY_TILE), g * HEAD_DIM:(g + 1) * HEAD_DIM], ones_col],
                                 axis=1)
            acc_ref[...] += jnp.dot(p, vt, preferred_element_type=f32)
            return carry

        sweep_tiles(values, group=4)
        acc = acc_ref[...]
        o = acc[:, :HEAD_DIM] / acc[:, HEAD_DIM:]
        for r in range(rep):
            hd = g * rep + r
            o_ref[:, hd * HEAD_DIM:(hd + 1) * HEAD_DIM] = o[r * tq:(r + 1) * tq].astype(o_ref.dtype)


def _dsa(q, qi, wi, kia, kib, kt, v, B, S):
    tq = min(Q_TILE, S)
    n_sel = min(TOPK_MAX, S // 4)
    nq = S // tq
    n_kt = S // KEY_TILE
    rep = N_HEADS // N_KV_HEADS
    row = lambda w: pl.BlockSpec((tq, w), lambda b, i: (b * nq + i, 0))
    once = pl.Buffered(1)
    kern = functools.partial(_dsa_kernel, tq=tq, n_sel=n_sel, idx_bits=max(1, (S - 1).bit_length()))
    return pl.pallas_call(
        kern,
        grid=(B, nq),
        in_specs=[row(ATTN_WIDTH), row(IDX_HEADS * IDX_DIM), row(LANES),
                  pl.BlockSpec((1, n_kt, LANES, KEY_TILE), lambda b, i: (b, 0, 0, 0), pipeline_mode=once),
                  pl.BlockSpec((1, n_kt, LANES, KEY_TILE), lambda b, i: (b, 0, 0, 0), pipeline_mode=once),
                  pl.BlockSpec((1, N_KV_HEADS, n_kt, HEAD_DIM, KEY_TILE), lambda b, i: (b, 0, 0, 0, 0),
                               pipeline_mode=once),
                  pl.BlockSpec((S, KV_WIDTH), lambda b, i: (b, 0), pipeline_mode=once)],
        out_specs=row(ATTN_WIDTH),
        out_shape=jax.ShapeDtypeStruct((B * S, ATTN_WIDTH), jnp.bfloat16),
        scratch_shapes=[pltpu.VMEM((n_kt, tq, KEY_TILE), jnp.int32),
                        pltpu.VMEM((n_kt, KEY_TILE, tq), jnp.int32),
                        pltpu.VMEM((IDX_HEADS, tq, LANES), jnp.float32),
                        pltpu.VMEM((rep * tq, LANES), jnp.float32),
                        pltpu.VMEM((rep * tq, 2 * HEAD_DIM), jnp.float32),
                        pltpu.VMEM((n_kt, rep * tq, KEY_TILE), jnp.float32)],
        compiler_params=_cparams(("arbitrary", "arbitrary")),
        name="dsa",
    )(q, qi, wi, kia, kib, kt, v)


_META_EID, _META_W, _META_RANK = 0, 2, 4


def _mix_kernel(attn_ref, p_ref, halo_ref, x_ref, ada_ref, wpool_ref, pscale_ref, wout_ref, g2_ref, wr_ref,
                x1_ref, h2_ref, meta_ref, counts_ref, pbuf_ref, run_ref, wrs_ref, *, tm, tiles_per_seq):
    i = pl.program_id(0)
    f32 = jnp.float32
    bf = jnp.bfloat16
    seq_tile = i % tiles_per_seq

    @pl.when(i == 0)
    def _():
        run_ref[...] = jnp.zeros(run_ref.shape, f32)
        wr = wr_ref[...]
        wr_hi = wr.astype(bf)
        wrs_ref[0] = wr_hi
        wrs_ref[1] = (wr - wr_hi.astype(f32)).astype(bf)

    p = p_ref[...]
    pbuf_ref[0:POOL_HALO, :] = jnp.where(seq_tile == 0, 0.0, halo_ref[...])
    pbuf_ref[POOL_HALO:POOL_HALO + tm, :] = p
    t_pos = seq_tile * tm + lax.broadcasted_iota(jnp.int32, (tm, 1), 0)
    pieces = [attn_ref[...]]
    for g, win in enumerate(POOL_WINDOWS):
        cs = slice(g * POOL_GROUP_DIM, (g + 1) * POOL_GROUP_DIM)
        tot = p[:, cs]
        for back in range(1, win):
            tot = tot + pbuf_ref[POOL_HALO - back:POOL_HALO - back + tm, cs]
        count = jnp.minimum(t_pos + 1, win).astype(f32)
        mixed = tot / count - p[:, cs]
        y = jnp.dot(mixed.astype(bf), wpool_ref[g], preferred_element_type=f32) * pscale_ref[:, cs]
        pieces.append(y.astype(bf))
    mix = jnp.dot(jnp.concatenate(pieces, axis=1), wout_ref[...], preferred_element_type=f32)
    x1 = x_ref[...] + ada_ref[0, 2:3, :] * mix
    x1_ref[...] = x1

    ms = jnp.mean(x1 * x1, axis=-1, keepdims=True)
    h2 = (x1 * lax.rsqrt(ms + NORM_EPS) * g2_ref[...]) * (1.0 + ada_ref[0, 4:5, :]) + ada_ref[0, 3:4, :]
    half = h2.shape[1] // 2
    h2_r = h2.astype(bf).astype(f32)
    hi_bits = lax.bitcast_convert_type(h2_r[:, :half], jnp.uint32)
    lo_bits = lax.bitcast_convert_type(h2_r[:, half:], jnp.uint32)
    h2_ref[...] = hi_bits | (lo_bits >> 16)

    h2_hi = h2.astype(bf)
    h2_lo = (h2 - h2_hi.astype(f32)).astype(bf)
    logits = (jnp.dot(h2_hi, wrs_ref[0], preferred_element_type=f32)
              + jnp.dot(h2_lo, wrs_ref[0], preferred_element_type=f32)
              + jnp.dot(h2_hi, wrs_ref[1], preferred_element_type=f32))
    lane = lax.broadcasted_iota(jnp.int32, (tm, LANES), 1)
    big = jnp.int32(LANES)
    rmax = lambda a: jnp.max(a, axis=-1, keepdims=True)
    rmin = lambda a: jnp.min(a, axis=-1, keepdims=True)
    rsum = lambda a: jnp.sum(a, axis=-1, keepdims=True)
    is_grp = lane < N_EXPERT_GROUPS
    m_g = rmax(jnp.where(is_grp, logits, -jnp.inf))
    p_g = 1.0 / rsum(jnp.where(is_grp, jnp.exp(logits - m_g), 0.0))
    g_sel = rmin(jnp.where(is_grp & (logits == m_g), lane, big))
    lo = N_EXPERT_GROUPS + EXPERTS_PER_GROUP * g_sel
    in_grp = (lane >= lo) & (lane < lo + EXPERTS_PER_GROUP)
    le = jnp.where(in_grp, logits, -jnp.inf)
    m_1 = rmax(le)
    i_1 = rmin(jnp.where(le == m_1, lane, big))
    le2 = jnp.where(lane == i_1, -jnp.inf, le)
    m_2 = rmax(le2)
    i_2 = rmin(jnp.where(le2 == m_2, lane, big))
    e_2 = jnp.exp(m_2 - m_1)
    w_1 = p_g / (1.0 + e_2)
    w_2 = p_g * e_2 / (1.0 + e_2)
    eid_1 = i_1 - N_EXPERT_GROUPS
    eid_2 = i_2 - N_EXPERT_GROUPS

    oh1 = lane == eid_1
    oh2 = lane == eid_2
    onehot = jnp.where(oh1, 1.0, 0.0) + jnp.where(oh2, 1.0, 0.0)
    r_i = lax.broadcasted_iota(jnp.int32, (tm, tm), 0)
    c_i = lax.broadcasted_iota(jnp.int32, (tm, tm), 1)
    earlier = jnp.where(c_i < r_i, 1.0, 0.0).astype(bf)
    before = jnp.dot(earlier, onehot.astype(bf), preferred_element_type=f32) + run_ref[...]
    rank_1 = rsum(jnp.where(oh1, before, 0.0))
    rank_2 = rsum(jnp.where(oh2, before, 0.0))
    run_ref[...] = run_ref[...] + jnp.sum(onehot, axis=0, keepdims=True)
    counts_ref[...] = run_ref[...]

    meta = jnp.zeros((tm, LANES), f32)
    for off, val in ((_META_EID, eid_1.astype(f32)), (_META_EID + 1, eid_2.astype(f32)),
                     (_META_W, w_1), (_META_W + 1, w_2), (_META_RANK, rank_1), (_META_RANK + 1, rank_2)):
        meta = jnp.where(lane == off, val, meta)
    meta_ref[...] = meta


def _mix(attn, p, x2d, ada3, w_pool, pool_scale, w_out, norm2_g, w_grp, w_exp, B, S):
    T, D = x2d.shape
    tm = min(256, S)
    tiles_per_seq = S // tm
    halo_blocks = tm // POOL_HALO
    bf = jnp.bfloat16
    w_router = jnp.concatenate(
        [w_grp, w_exp, jnp.zeros((D, LANES - N_EXPERT_GROUPS - N_EXPERTS), w_grp.dtype)], axis=1)
    row = lambda w: pl.BlockSpec((tm, w), lambda i: (i, 0))
    const = lambda shape: pl.BlockSpec(shape, lambda i: (0,) * len(shape))
    once = lambda shape: pl.BlockSpec(shape, lambda i: (0,) * len(shape), pipeline_mode=pl.Buffered(1))
    kern = functools.partial(_mix_kernel, tm=tm, tiles_per_seq=tiles_per_seq)
    return pl.pallas_call(
        kern,
        grid=(T // tm,),
        in_specs=[row(ATTN_WIDTH), row(POOL_WIDTH),
                  pl.BlockSpec((POOL_HALO, POOL_WIDTH), lambda i: (jnp.maximum(i * halo_blocks - 1, 0), 0)),
                  row(D),
                  pl.BlockSpec((1, 6, D), lambda i: (i // tiles_per_seq, 0, 0)),
                  once((POOL_GROUPS, POOL_GROUP_DIM, POOL_GROUP_DIM)), const((1, POOL_WIDTH)),
                  once((ATTN_WIDTH + POOL_WIDTH, D)), const((1, D)), once((D, LANES))],
        out_specs=[row(D), row(D // 2), row(LANES), const((1, LANES))],
        out_shape=[jax.ShapeDtypeStruct((T, D), jnp.float32),
                   jax.ShapeDtypeStruct((T, D // 2), jnp.uint32),
                   jax.ShapeDtypeStruct((T, LANES), jnp.float32),
                   jax.ShapeDtypeStruct((1, LANES), jnp.float32)],
        scratch_shapes=[pltpu.VMEM((POOL_HALO + tm, POOL_WIDTH), jnp.float32),
                        pltpu.VMEM((1, LANES), jnp.float32),
                        pltpu.VMEM((2, D, LANES), bf)],
        compiler_params=_cparams(("arbitrary",)),
        name="mix",
    )(attn, p, p, x2d, ada3, w_pool.astype(bf), pool_scale.reshape(1, POOL_WIDTH), w_out.astype(bf),
      norm2_g.reshape(1, D), w_router)


def _dispatch_kernel(pad_start_ref, pad_len_ref, n_used_ref, pos_ref, h2_ref, xs_ref, zeros_ref, sem, *, rt, n_tiles):
    i = pl.program_id(0)

    @pl.when(i == 0)
    def _():
        zeros_ref[...] = jnp.zeros(zeros_ref.shape, zeros_ref.dtype)

        def zero_row(r):
            return pltpu.make_async_copy(zeros_ref.at[pl.ds(0, 1)], xs_ref.at[pl.ds(r, 1)], sem)

        def zero_tile(t):
            start = pl.multiple_of(t * EXPERT_TILE, EXPERT_TILE)
            return pltpu.make_async_copy(zeros_ref, xs_ref.at[pl.ds(start, EXPERT_TILE)], sem)

        def fill(e, carry):
            start = pad_start_ref[e]
            n = pad_len_ref[e]
            lax.fori_loop(0, n, lambda r, c: (zero_row(start + r).start(), c)[1], 0)
            lax.fori_loop(0, n, lambda r, c: (zero_row(0).wait(), c)[1], 0)
            return carry

        lax.fori_loop(0, N_EXPERTS, fill, 0)
        first_unused = n_used_ref[0]
        lax.fori_loop(first_unused, n_tiles, lambda t, c: (zero_tile(t).start(), c)[1], 0)
        lax.fori_loop(first_unused, n_tiles, lambda t, c: (zero_tile(0).wait(), c)[1], 0)

    def row_copy(t, slot):
        return pltpu.make_async_copy(h2_ref.at[pl.ds(t, 1)], xs_ref.at[pl.ds(pos_ref[0, 0, 2 * t + slot], 1)], sem)

    def issue(t, carry):
        row_copy(t, 0).start(priority=0)
        row_copy(t, 1).start(priority=1)
        return carry

    def drain(t, carry):
        for _ in range(2):
            pltpu.make_async_copy(h2_ref.at[pl.ds(0, 1)], xs_ref.at[pl.ds(0, 1)], sem).wait()
        return carry

    lax.fori_loop(0, rt, issue, 0)
    lax.fori_loop(0, rt, drain, 0)


def _dispatch(h2, pos, pad_start, pad_len, n_used, n_rows):
    T, D = h2.shape
    rt = min(ROW_TILE, T)
    kern = functools.partial(_dispatch_kernel, rt=rt, n_tiles=n_rows // EXPERT_TILE)
    return pl.pallas_call(
        kern,
        grid_spec=pltpu.PrefetchScalarGridSpec(
            num_scalar_prefetch=3,
            grid=(T // rt,),
            in_specs=[pl.BlockSpec((1, 1, 2 * rt), lambda i, *_: (i, 0, 0), memory_space=pltpu.SMEM),
                      pl.BlockSpec((rt, D), lambda i, *_: (i, 0))],
            out_specs=pl.BlockSpec(memory_space=pl.ANY),
            scratch_shapes=[pltpu.VMEM((EXPERT_TILE, D), h2.dtype), pltpu.SemaphoreType.DMA(())]),
        out_shape=jax.ShapeDtypeStruct((n_rows, D), h2.dtype),
        compiler_params=_cparams(("arbitrary",)),
        name="dispatch",
    )(pad_start, pad_len, n_used, pos.reshape(T // rt, 1, 2 * rt), h2)


def _experts_kernel(tile_expert_ref, n_used_ref, xs_ref, w1_ref, w3_ref, w2_ref, ys_ref, w1b_ref, w3b_ref, w2b_ref):
    i = pl.program_id(0)
    bf = jnp.bfloat16

    @pl.when((i == 0) | (tile_expert_ref[i] != tile_expert_ref[jnp.maximum(i - 1, 0)]))
    def _():
        w1b_ref[...] = w1_ref[0].astype(bf)
        w3b_ref[...] = w3_ref[0].astype(bf)
        w2b_ref[...] = w2_ref[0].astype(bf)

    @pl.when(i < n_used_ref[0])
    def _():
        packed = xs_ref[...]
        xb = jnp.concatenate([lax.bitcast_convert_type(packed & jnp.uint32(0xFFFF0000), jnp.float32),
                              lax.bitcast_convert_type(packed << 16, jnp.float32)], axis=1).astype(bf)
        a1 = jnp.dot(xb, w1b_ref[...], preferred_element_type=jnp.float32)
        a3 = jnp.dot(xb, w3b_ref[...], preferred_element_type=jnp.float32)
        act = (a1 * (1.0 / (1.0 + jnp.exp(-a1))) * a3).astype(bf)
        ys_ref[...] = jnp.dot(act, w2b_ref[...], preferred_element_type=jnp.float32)

    @pl.when(i >= n_used_ref[0])
    def _():
        ys_ref[...] = jnp.zeros(ys_ref.shape, ys_ref.dtype)


def _experts(xs, w1, w3, w2, tile_expert, n_used):
    n_rows = xs.shape[0]
    D = w1.shape[1]
    n_tiles = n_rows // EXPERT_TILE
    bf = jnp.bfloat16
    return pl.pallas_call(
        _experts_kernel,
        grid_spec=pltpu.PrefetchScalarGridSpec(
            num_scalar_prefetch=2,
            grid=(n_tiles,),
            in_specs=[pl.BlockSpec((EXPERT_TILE, D // 2), lambda i, te, nu: (jnp.minimum(i, nu[0] - 1), 0)),
                      pl.BlockSpec((1, D, D_EXPERT), lambda i, te, nu: (te[i], 0, 0)),
                      pl.BlockSpec((1, D, D_EXPERT), lambda i, te, nu: (te[i], 0, 0)),
                      pl.BlockSpec((1, D_EXPERT, D), lambda i, te, nu: (te[i], 0, 0))],
            out_specs=pl.BlockSpec((EXPERT_TILE, D), lambda i, te, nu: (i, 0)),
            scratch_shapes=[pltpu.VMEM((D, D_EXPERT), bf), pltpu.VMEM((D, D_EXPERT), bf),
                            pltpu.VMEM((D_EXPERT, D), bf)]),
        out_shape=jax.ShapeDtypeStruct((n_rows, D), jnp.float32),
        compiler_params=_cparams(("arbitrary",)),
        name="experts",
    )(tile_expert, n_used, xs, w1, w3, w2)


def _combine_kernel(pos_ref, ys_ref, x1_ref, meta_ref, ada_ref, o_ref, rows_ref, sem, *, rt):
    def row_copy(t, slot):
        return pltpu.make_async_copy(ys_ref.at[pl.ds(pos_ref[0, 0, 2 * t + slot], 1)],
                                     rows_ref.at[slot, pl.ds(t, 1)], sem)

    def issue(t, carry):
        row_copy(t, 0).start(priority=0)
        row_copy(t, 1).start(priority=1)
        return carry

    def drain(t, carry):
        for slot in range(2):
            pltpu.make_async_copy(ys_ref.at[pl.ds(0, 1)], rows_ref.at[slot, pl.ds(0, 1)], sem).wait()
        return carry

    lax.fori_loop(0, rt, issue, 0)
    lax.fori_loop(0, rt, drain, 0)
    meta = meta_ref[...]
    y = meta[:, _META_W:_META_W + 1] * rows_ref[0] + meta[:, _META_W + 1:_META_W + 2] * rows_ref[1]
    o_ref[...] = x1_ref[...] + ada_ref[0, 5:6, :] * y


def _combine(ys, pos, x1, meta, ada3, S):
    T, D = x1.shape
    rt = min(ROW_TILE, S)
    tiles_per_seq = S // rt
    kern = functools.partial(_combine_kernel, rt=rt)
    return pl.pallas_call(
        kern,
        grid=(T // rt,),
        in_specs=[pl.BlockSpec((1, 1, 2 * rt), lambda i: (i, 0, 0), memory_space=pltpu.SMEM),
                  pl.BlockSpec(memory_space=pl.ANY),
                  pl.BlockSpec((rt, D), lambda i: (i, 0)),
                  pl.BlockSpec((rt, LANES), lambda i: (i, 0)),
                  pl.BlockSpec((1, 6, D), lambda i: (i // tiles_per_seq, 0, 0))],
        out_specs=pl.BlockSpec((rt, D), lambda i: (i, 0)),
        out_shape=jax.ShapeDtypeStruct((T, D), jnp.float32),
        scratch_shapes=[pltpu.VMEM((2, rt, D), jnp.float32), pltpu.SemaphoreType.DMA(())],
        compiler_params=_cparams(("arbitrary",)),
        name="combine",
    )(pos.reshape(T // rt, 1, 2 * rt), ys, x1, meta, ada3)


def _routing_plan(meta, counts, T):
    i32 = jnp.int32
    eid = meta[:, _META_EID:_META_EID + 2].reshape(-1).astype(i32)
    rank = meta[:, _META_RANK:_META_RANK + 2].reshape(-1).astype(i32)
    cnt = counts[0, :N_EXPERTS].astype(i32)
    padded = (cnt + EXPERT_TILE - 1) // EXPERT_TILE * EXPERT_TILE
    ends = jnp.cumsum(padded)
    starts = ends - padded
    pos = starts[eid] + rank
    n_rows = 2 * T + N_EXPERTS * EXPERT_TILE
    n_tiles = n_rows // EXPERT_TILE
    n_used = (ends[-1] // EXPERT_TILE).astype(i32)
    tile_start = jnp.minimum(jnp.arange(n_tiles, dtype=i32), n_used - 1) * EXPERT_TILE
    tile_expert = jnp.sum((ends[None, :] <= tile_start[:, None]).astype(i32), axis=1)
    return pos, starts + cnt, padded - cnt, tile_expert, n_used.reshape(1), n_rows


def kernel(x, c, w_ada, b_ada, norm1_g, w_in, q_norm_g, k_norm_g, w_pool, pool_scale,
           w_out, norm2_g, w_grp, w_exp, w1, w3, w2):
    B, S, D = x.shape
    T = B * S
    x2d = x.reshape(T, D)
    for l in range(w_ada.shape[0]):
        ada3 = _ada(c, w_ada[l], b_ada[l]).reshape(B, 6, D)
        q, kt, v, qi, kia, kib, wi, p = _inproj(x2d, ada3, norm1_g[l], w_in[l], q_norm_g[l], k_norm_g[l], B, S)
        attn = _dsa(q, qi, wi, kia, kib, kt, v, B, S)
        x1, h2, meta, counts = _mix(attn, p, x2d, ada3, w_pool[l], pool_scale[l], w_out[l], norm2_g[l],
                                    w_grp[l], w_exp[l], B, S)
        pos, pad_start, pad_len, tile_expert, n_used, n_rows = _routing_plan(meta, counts, T)
        xs = _dispatch(h2, pos, pad_start, pad_len, n_used, n_rows)
        ys = _experts(xs, w1[l], w3[l], w2[l], tile_expert, n_used)
        x2d = _combine(ys, pos, x1, meta, ada3, S)
    return x2d.reshape(B, S, D)
```

```python
import functools

import numpy as np
import jax
import jax.numpy as jnp
from jax import lax
from jax.experimental import pallas as pl
from jax.experimental.pallas import tpu as pltpu

N_HEADS = 8
N_KV_HEADS = 2
HEAD_DIM = 128
ATTN_WIDTH = N_HEADS * HEAD_DIM
KV_WIDTH = N_KV_HEADS * HEAD_DIM
ROPE_THETA = 500000.0
ROPE_FRACTION = 4
IDX_HEADS = 16
IDX_DIM = 64
TOPK_MAX = 256
POOL_GROUPS = 4
POOL_WINDOWS = (2, 4, 8, 16)
POOL_GROUP_DIM = 256
POOL_WIDTH = POOL_GROUPS * POOL_GROUP_DIM
N_EXPERT_GROUPS = 4
EXPERTS_PER_GROUP = 8
N_EXPERTS = N_EXPERT_GROUPS * EXPERTS_PER_GROUP
D_EXPERT = 512
NORM_EPS = 1e-6

LANES = 128
VMEM_LIMIT_BYTES = 56 * 1024 * 1024

KEY_TILE = 512
Q_TILE = 128
POOL_HALO = 16
EXPERT_TILE = 256
ROW_TILE = 512

_C_Q = 0
_C_K = _C_Q + ATTN_WIDTH
_C_V = _C_K + KV_WIDTH
_C_QI = _C_V + KV_WIDTH
_C_KI = _C_QI + IDX_HEADS * IDX_DIM
_C_WI = _C_KI + LANES
_C_P = _C_WI + LANES
_C_END = _C_P + POOL_WIDTH

_NEG = -1e30
_LOG2_E = 1.4426950408889634
_INT_MIN = -2 ** 31
_KEY_LOWEST_FINITE = -2 ** 31 + 0x00800000
_BISECT_FIXED_PASSES = 22


def _cparams(semantics):
    return pltpu.CompilerParams(dimension_semantics=semantics, vmem_limit_bytes=VMEM_LIMIT_BYTES)


def _ada_kernel(c_ref, w_ref, b_ref, o_ref):
    c = c_ref[...]
    s = c * (1.0 / (1.0 + jnp.exp(-c)))
    o_ref[...] = jnp.dot(s, w_ref[...], preferred_element_type=jnp.float32,
                         precision=lax.Precision.HIGHEST) + b_ref[...]


def _ada(c, w_ada, b_ada):
    B, D = c.shape
    N = w_ada.shape[1]
    tn = 1024
    rows = 8
    c_pad = jnp.zeros((rows, D), jnp.float32).at[:B].set(c)
    out = pl.pallas_call(
        _ada_kernel,
        grid=(N // tn,),
        in_specs=[pl.BlockSpec((rows, D), lambda j: (0, 0)),
                  pl.BlockSpec((D, tn), lambda j: (0, j)),
                  pl.BlockSpec((1, tn), lambda j: (0, j))],
        out_specs=pl.BlockSpec((rows, tn), lambda j: (0, j)),
        out_shape=jax.ShapeDtypeStruct((rows, N), jnp.float32),
        compiler_params=_cparams(("arbitrary",)),
        name="ada",
    )(c_pad, w_ada, b_ada.reshape(1, N))
    return out[:B]


def _rope_tables(S, head_dim):
    rd = head_dim // ROPE_FRACTION
    half = rd // 2
    pos = jnp.arange(S, dtype=jnp.float32)
    inv = jnp.float32(ROPE_THETA) ** (-(jnp.arange(half, dtype=jnp.float32) * 2.0) / rd)
    ang = pos[:, None] * inv[None, :]
    cos, sin = jnp.cos(ang), jnp.sin(ang)
    lane = np.arange(LANES) % head_dim
    fidx = np.where(lane < half, lane, lane - half) % half
    in_lo = jnp.asarray(lane < half)[None, :]
    in_hi = jnp.asarray((lane >= half) & (lane < rd))[None, :]
    cos_l, sin_l = cos[:, fidx], sin[:, fidx]
    c_tab = jnp.where(in_lo | in_hi, cos_l, 1.0)
    s1_tab = jnp.where(in_lo, -sin_l, 0.0)
    s2_tab = jnp.where(in_hi, sin_l, 0.0)
    return c_tab, s1_tab, s2_tab, half


def _rope(x, c_tab, s1_tab, s2_tab, half):
    return (x * c_tab + pltpu.roll(x, LANES - half, 1) * s1_tab + pltpu.roll(x, half, 1) * s2_tab)


def _inproj_kernel(x_ref, ada_ref, g1_ref, w_ref, qg_ref, kg_ref,
                   cq_ref, s1q_ref, s2q_ref, ci_ref, s1i_ref, s2i_ref,
                   q_ref, kt_ref, v_ref, qi_ref, kia_ref, kib_ref, wi_ref, p_ref,
                   *, half_qk, half_idx, n_chunks):
    x = x_ref[...]
    ms = jnp.mean(x * x, axis=-1, keepdims=True)
    xn = x * lax.rsqrt(ms + NORM_EPS) * g1_ref[...]
    h = (xn * (1.0 + ada_ref[0, 1:2, :]) + ada_ref[0, 0:1, :]).astype(jnp.bfloat16)

    def proj(lo, hi):
        return jnp.dot(h, w_ref[:, lo:hi], preferred_element_type=jnp.float32)

    cq, s1q, s2q = cq_ref[...], s1q_ref[...], s2q_ref[...]
    ci, s1i, s2i = ci_ref[...], s1i_ref[...], s2i_ref[...]

    def qk_head(slab, gain):
        m = jnp.mean(slab * slab, axis=-1, keepdims=True)
        y = slab * lax.rsqrt(m + NORM_EPS) * gain
        return _rope(y, cq, s1q, s2q, half_qk)

    attn_scale = HEAD_DIM ** -0.5 * _LOG2_E
    q = proj(_C_Q, _C_K)
    for hd in range(N_HEADS):
        sl = slice(hd * HEAD_DIM, (hd + 1) * HEAD_DIM)
        q_ref[:, sl] = (qk_head(q[:, sl], qg_ref[...]) * attn_scale).astype(jnp.bfloat16)

    k = proj(_C_K, _C_V)
    for g in range(N_KV_HEADS):
        kt = qk_head(k[:, g * HEAD_DIM:(g + 1) * HEAD_DIM], kg_ref[...]).T.astype(jnp.bfloat16)
        for c in range(n_chunks):
            kt_ref[0, g, c] = kt[:, c * KEY_TILE:(c + 1) * KEY_TILE]

    v_ref[...] = proj(_C_V, _C_QI).astype(jnp.bfloat16)

    qi = proj(_C_QI, _C_KI)
    for j in range(IDX_HEADS * IDX_DIM // LANES):
        sl = slice(j * LANES, (j + 1) * LANES)
        qi_ref[:, sl] = _rope(qi[:, sl], ci, s1i, s2i, half_idx).astype(jnp.bfloat16)

    ki = _rope(proj(_C_KI, _C_WI), ci, s1i, s2i, half_idx).T
    ki_swapped = jnp.concatenate([ki[IDX_DIM:], ki[:IDX_DIM]], axis=0)
    for c in range(n_chunks):
        kia_ref[0, c] = ki[:, c * KEY_TILE:(c + 1) * KEY_TILE].astype(jnp.bfloat16)
        kib_ref[0, c] = ki_swapped[:, c * KEY_TILE:(c + 1) * KEY_TILE].astype(jnp.bfloat16)

    idx_scale = (IDX_DIM ** -0.5) * (IDX_HEADS ** -0.5)
    wi_ref[...] = proj(_C_WI, _C_P) * idx_scale
    p_ref[...] = proj(_C_P, _C_END)


def _inproj(x2d, ada3, norm1_g, w_in, q_norm_g, k_norm_g, B, S):
    T, D = x2d.shape
    tm = min(512, S)
    n_chunks = tm // KEY_TILE
    tiles_per_seq = S // tm
    bf = jnp.bfloat16

    offs = np.cumsum((ATTN_WIDTH, KV_WIDTH, KV_WIDTH, IDX_HEADS * IDX_DIM, IDX_DIM, IDX_HEADS))
    o_q, o_k, o_v, o_qi, o_ki, o_wi = [int(o) for o in offs]
    zeros = lambda n: jnp.zeros((D, n), w_in.dtype)
    w_pad = jnp.concatenate([
        w_in[:, :o_qi],
        w_in[:, o_qi:o_ki], zeros(LANES - IDX_DIM),
        w_in[:, o_ki:o_wi], zeros(LANES - IDX_HEADS),
        w_in[:, o_wi:],
    ], axis=1).astype(bf)
    assert w_pad.shape[1] == _C_END

    cq, s1q, s2q, half_qk = _rope_tables(S, HEAD_DIM)
    ci, s1i, s2i, half_idx = _rope_tables(S, IDX_DIM)

    row = lambda w: pl.BlockSpec((tm, w), lambda i: (i, 0))
    tab = pl.BlockSpec((tm, LANES), lambda i: (i % tiles_per_seq, 0))
    const = lambda shape: pl.BlockSpec(shape, lambda i: (0,) * len(shape))
    kern = functools.partial(_inproj_kernel, half_qk=half_qk, half_idx=half_idx, n_chunks=n_chunks)
    return pl.pallas_call(
        kern,
        grid=(T // tm,),
        in_specs=[row(D),
                  pl.BlockSpec((1, 6, D), lambda i: (i // tiles_per_seq, 0, 0)),
                  const((1, D)),
                  pl.BlockSpec((D, _C_END), lambda i: (0, 0), pipeline_mode=pl.Buffered(1)),
                  const((1, HEAD_DIM)), const((1, HEAD_DIM)),
                  tab, tab, tab, tab, tab, tab],
        out_specs=[row(ATTN_WIDTH),
                   pl.BlockSpec((1, N_KV_HEADS, n_chunks, HEAD_DIM, KEY_TILE),
                                lambda i: (i // tiles_per_seq, 0, i % tiles_per_seq, 0, 0)),
                   row(KV_WIDTH),
                   row(IDX_HEADS * IDX_DIM),
                   pl.BlockSpec((1, n_chunks, LANES, KEY_TILE),
                                lambda i: (i // tiles_per_seq, i % tiles_per_seq, 0, 0)),
                   pl.BlockSpec((1, n_chunks, LANES, KEY_TILE),
                                lambda i: (i // tiles_per_seq, i % tiles_per_seq, 0, 0)),
                   row(LANES),
                   row(POOL_WIDTH)],
        out_shape=[jax.ShapeDtypeStruct((T, ATTN_WIDTH), bf),
                   jax.ShapeDtypeStruct((B, N_KV_HEADS, S // KEY_TILE, HEAD_DIM, KEY_TILE), bf),
                   jax.ShapeDtypeStruct((T, KV_WIDTH), bf),
                   jax.ShapeDtypeStruct((T, IDX_HEADS * IDX_DIM), bf),
                   jax.ShapeDtypeStruct((B, S // KEY_TILE, LANES, KEY_TILE), bf),
                   jax.ShapeDtypeStruct((B, S // KEY_TILE, LANES, KEY_TILE), bf),
                   jax.ShapeDtypeStruct((T, LANES), jnp.float32),
                   jax.ShapeDtypeStruct((T, POOL_WIDTH), jnp.float32)],
        compiler_params=_cparams(("arbitrary",)),
        name="inproj",
    )(x2d, ada3, norm1_g.reshape(1, D), w_pad, q_norm_g.reshape(1, HEAD_DIM), k_norm_g.reshape(1, HEAD_DIM),
      cq, s1q, s2q, ci, s1i, s2i)


def _dsa_kernel(q_ref, qi_ref, wi_ref, kia_ref, kib_ref, kt_ref, v_ref, o_ref,
                keys_ref, keyst_ref, wb_ref, m_ref, acc_ref, s_ref, *, tq, n_sel, idx_bits):
    qb = pl.program_id(1)
    n_tiles = (qb * tq + tq + KEY_TILE - 1) // KEY_TILE
    lane_chunks = KEY_TILE // LANES
    rep = N_HEADS // N_KV_HEADS
    f32 = jnp.float32

    qi = qi_ref[...]
    n_slabs = IDX_HEADS * IDX_DIM // LANES
    lhs = jnp.concatenate([qi[:, j * LANES:(j + 1) * LANES] for j in range(n_slabs)], axis=0)
    w = wi_ref[...]
    for hd in range(IDX_HEADS):
        wb_ref[hd] = jnp.broadcast_to(w[:, hd:hd + 1], (tq, LANES))
    t_pos = qb * tq + lax.broadcasted_iota(jnp.int32, (tq, KEY_TILE), 0)
    k_off = lax.broadcasted_iota(jnp.int32, (tq, KEY_TILE), 1)

    def sweep_tiles(body, init=0, group=2):
        shift = group.bit_length() - 1
        n_groups = lax.shift_right_logical(n_tiles, shift)

        def grouped(i, carry):
            for u in range(group):
                carry = body(group * i + u, carry)
            return carry

        carry = lax.fori_loop(0, n_groups, grouped, init)
        return lax.fori_loop(group * n_groups, n_tiles, body, carry)

    def score_tile(j, carry):
        za = jnp.dot(lhs, kia_ref[0, j], preferred_element_type=f32)
        zb = jnp.dot(lhs, kib_ref[0, j], preferred_element_type=f32)
        acc = jnp.zeros((tq, KEY_TILE), f32)
        for s in range(n_slabs):
            wa = jnp.concatenate([wb_ref[2 * s]] * lane_chunks, axis=1)
            wo = jnp.concatenate([wb_ref[2 * s + 1]] * lane_chunks, axis=1)
            acc = acc + wa * jnp.maximum(za[s * tq:(s + 1) * tq], 0.0)
            acc = acc + wo * jnp.maximum(zb[s * tq:(s + 1) * tq], 0.0)
        score = jnp.where(j * KEY_TILE + k_off <= t_pos, acc, -jnp.inf)
        bits = lax.bitcast_convert_type(score, jnp.int32)
        key = bits ^ ((bits >> 31) & 0x7FFFFFFF)
        keys_ref[j] = key
        for cc in range(lane_chunks):
            keyst_ref[j, cc * LANES:(cc + 1) * LANES, :] = key[:, cc * LANES:(cc + 1) * LANES].T
        return carry

    sweep_tiles(score_tile)

    def rows_to_lanes_sum(hit):
        return jnp.sum(hit.reshape(KEY_TILE // 8, 8, tq), axis=0)

    def count_ge(cand):
        def body(j, cnt):
            return cnt + rows_to_lanes_sum(jnp.where(keyst_ref[j] >= cand, 1, 0))
        cnt = sweep_tiles(body, jnp.zeros((8, tq), jnp.int32), group=4)
        return jnp.sum(cnt, axis=0, keepdims=True)

    def bisect(i, state):
        thr, at_thr = state
        cand = thr + lax.shift_left(jnp.int32(1), 31 - i)
        cnt = count_ge(cand)
        take = cnt >= n_sel
        return jnp.where(take, cand, thr), jnp.where(take, cnt, at_thr)

    state = (jnp.full((1, tq), _INT_MIN, jnp.int32), jnp.full((1, tq), 2 ** 31 - 1, jnp.int32))
    state = lax.fori_loop(0, _BISECT_FIXED_PASSES, bisect, state)

    def unsettled(carry):
        i, _, at_thr = carry
        return (i < 32) & (jnp.max(jnp.where(at_thr == n_sel, 0, 1)) > 0)

    def two_more_bits(carry):
        i, thr, at_thr = carry
        thr, at_thr = bisect(i + 1, bisect(i, (thr, at_thr)))
        return i + 2, thr, at_thr

    _, thr_t, at_thr_t = lax.while_loop(unsettled, two_more_bits, (jnp.int32(_BISECT_FIXED_PASSES),) + state)

    def per_query_rows(vec_t):
        rows = jnp.broadcast_to(vec_t, (tq, tq)).T
        return jnp.concatenate([rows] * lane_chunks, axis=1)

    thr_wide = per_query_rows(thr_t)

    tied = (at_thr_t > n_sel) & (thr_t >= _KEY_LOWEST_FINITE)
    any_tied = jnp.max(jnp.where(tied, 1, 0)) > 0

    @pl.when(any_tied)
    def _():
        need = n_sel - count_ge(thr_t + 1)
        row_idx = lax.broadcasted_iota(jnp.int32, (KEY_TILE, tq), 0)

        def tied_before(limit):
            def body(j, cnt):
                hit = jnp.where(j * KEY_TILE + row_idx < limit, 1, 0)
                return cnt + rows_to_lanes_sum(jnp.where(keyst_ref[j] == thr_t, hit, 0))
            cnt = lax.fori_loop(0, n_tiles, body, jnp.zeros((8, tq), jnp.int32))
            return jnp.sum(cnt, axis=0, keepdims=True)

        def bisect_index(i, last):
            cand = last + lax.shift_left(jnp.int32(1), idx_bits - 1 - i)
            return jnp.where(tied_before(cand) < need, cand, last)

        last_t = lax.fori_loop(0, idx_bits, bisect_index, jnp.zeros((1, tq), jnp.int32))
        last_wide = per_query_rows(last_t)

        def demote(j, carry):
            kk = keys_ref[j]
            drop = jnp.where(j * KEY_TILE + k_off > last_wide, thr_wide - 1, kk)
            keys_ref[j] = jnp.where(kk == thr_wide, drop, kk)
            return carry

        lax.fori_loop(0, n_tiles, demote, 0)

    thr_wide = jnp.maximum(thr_wide, _KEY_LOWEST_FINITE)

    q = q_ref[...]
    ones_col = jnp.ones((KEY_TILE, HEAD_DIM), jnp.bfloat16)
    for g in range(N_KV_HEADS):
        qg = jnp.concatenate([q[:, (g * rep + r) * HEAD_DIM:(g * rep + r + 1) * HEAD_DIM] for r in range(rep)],
                             axis=0)
        m_ref[...] = jnp.full(m_ref.shape, _NEG, f32)
        acc_ref[...] = jnp.zeros(acc_ref.shape, f32)

        def scores(j, carry, qg=qg, g=g):
            bias = jnp.where(keys_ref[j] >= thr_wide, 0.0, _NEG)
            s = jnp.dot(qg, kt_ref[0, g, j], preferred_element_type=f32) + jnp.concatenate([bias] * rep, axis=0)
            s_ref[j] = s
            m_part = m_ref[...]
            for cc in range(lane_chunks):
                m_part = jnp.maximum(m_part, s[:, cc * LANES:(cc + 1) * LANES])
            m_ref[...] = m_part
            return carry

        sweep_tiles(scores, group=4)
        m_ref[...] = jnp.broadcast_to(jnp.max(m_ref[...], axis=-1, keepdims=True), m_ref.shape)

        def values(j, carry, g=g):
            m_row = m_ref[...]
            p = jnp.exp2(s_ref[j] - jnp.concatenate([m_row] * lane_chunks, axis=1)).astype(jnp.bfloat16)
            start = pl.multiple_of(j * KEY_TILE, KEY_TILE)
            vt = jnp.concatenate([v_ref[pl.ds(start, KEY_TILE), g * HEAD_DIM:(g + 1) * HEAD_DIM], ones_col],
                                 axis=1)
            acc_ref[...] += jnp.dot(p, vt, preferred_element_type=f32)
            return carry

        sweep_tiles(values, group=4)
        acc = acc_ref[...]
        o = acc[:, :HEAD_DIM] / acc[:, HEAD_DIM:]
        for r in range(rep):
            hd = g * rep + r
            o_ref[:, hd * HEAD_DIM:(hd + 1) * HEAD_DIM] = o[r * tq:(r + 1) * tq].astype(o_ref.dtype)


def _dsa(q, qi, wi, kia, kib, kt, v, B, S):
    tq = min(Q_TILE, S)
    n_sel = min(TOPK_MAX, S // 4)
    nq = S // tq
    n_kt = S // KEY_TILE
    rep = N_HEADS // N_KV_HEADS
    row = lambda w: pl.BlockSpec((tq, w), lambda b, i: (b * nq + i, 0))
    once = pl.Buffered(1)
    kern = functools.partial(_dsa_kernel, tq=tq, n_sel=n_sel, idx_bits=max(1, (S - 1).bit_length()))
    return pl.pallas_call(
        kern,
        grid=(B, nq),
        in_specs=[row(ATTN_WIDTH), row(IDX_HEADS * IDX_DIM), row(LANES),
                  pl.BlockSpec((1, n_kt, LANES, KEY_TILE), lambda b, i: (b, 0, 0, 0), pipeline_mode=once),
                  pl.BlockSpec((1, n_kt, LANES, KEY_TILE), lambda b, i: (b, 0, 0, 0), pipeline_mode=once),
                  pl.BlockSpec((1, N_KV_HEADS, n_kt, HEAD_DIM, KEY_TILE), lambda b, i: (b, 0, 0, 0, 0),
                               pipeline_mode=once),
                  pl.BlockSpec((S, KV_WIDTH), lambda b, i: (b, 0), pipeline_mode=once)],
        out_specs=row(ATTN_WIDTH),
        out_shape=jax.ShapeDtypeStruct((B * S, ATTN_WIDTH), jnp.bfloat16),
        scratch_shapes=[pltpu.VMEM((n_kt, tq, KEY_TILE), jnp.int32),
                        pltpu.VMEM((n_kt, KEY_TILE, tq), jnp.int32),
                        pltpu.VMEM((IDX_HEADS, tq, LANES), jnp.float32),
                        pltpu.VMEM((rep * tq, LANES), jnp.float32),
                        pltpu.VMEM((rep * tq, 2 * HEAD_DIM), jnp.float32),
                        pltpu.VMEM((n_kt, rep * tq, KEY_TILE), jnp.float32)],
        compiler_params=_cparams(("arbitrary", "arbitrary")),
        name="dsa",
    )(q, qi, wi, kia, kib, kt, v)


_META_EID, _META_W, _META_RANK = 0, 2, 4


def _mix_kernel(attn_ref, p_ref, halo_ref, x_ref, ada_ref, wpool_ref, pscale_ref, wout_ref, g2_ref, wr_ref,
                x1_ref, h2_ref, meta_ref, counts_ref, pbuf_ref, run_ref, wrs_ref, *, tm, tiles_per_seq):
    i = pl.program_id(0)
    f32 = jnp.float32
    bf = jnp.bfloat16
    seq_tile = i % tiles_per_seq

    @pl.when(i == 0)
    def _():
        run_ref[...] = jnp.zeros(run_ref.shape, f32)
        wr = wr_ref[...]
        wr_hi = wr.astype(bf)
        wrs_ref[0] = wr_hi
        wrs_ref[1] = (wr - wr_hi.astype(f32)).astype(bf)

    p = p_ref[...]
    pbuf_ref[0:POOL_HALO, :] = jnp.where(seq_tile == 0, 0.0, halo_ref[...])
    pbuf_ref[POOL_HALO:POOL_HALO + tm, :] = p
    t_pos = seq_tile * tm + lax.broadcasted_iota(jnp.int32, (tm, 1), 0)
    pieces = [attn_ref[...]]
    for g, win in enumerate(POOL_WINDOWS):
        cs = slice(g * POOL_GROUP_DIM, (g + 1) * POOL_GROUP_DIM)
        tot = p[:, cs]
        for back in range(1, win):
            tot = tot + pbuf_ref[POOL_HALO - back:POOL_HALO - back + tm, cs]
        count = jnp.minimum(t_pos + 1, win).astype(f32)
        mixed = tot / count - p[:, cs]
        y = jnp.dot(mixed.astype(bf), wpool_ref[g], preferred_element_type=f32) * pscale_ref[:, cs]
        pieces.append(y.astype(bf))
    mix = jnp.dot(jnp.concatenate(pieces, axis=1), wout_ref[...], preferred_element_type=f32)
    x1 = x_ref[...] + ada_ref[0, 2:3, :] * mix
    x1_ref[...] = x1

    ms = jnp.mean(x1 * x1, axis=-1, keepdims=True)
    h2 = (x1 * lax.rsqrt(ms + NORM_EPS) * g2_ref[...]) * (1.0 + ada_ref[0, 4:5, :]) + ada_ref[0, 3:4, :]
    h2_ref[...] = h2

    h2_hi = h2.astype(bf)
    h2_lo = (h2 - h2_hi.astype(f32)).astype(bf)
    logits = (jnp.dot(h2_hi, wrs_ref[0], preferred_element_type=f32)
              + jnp.dot(h2_lo, wrs_ref[0], preferred_element_type=f32)
              + jnp.dot(h2_hi, wrs_ref[1], preferred_element_type=f32))
    lane = lax.broadcasted_iota(jnp.int32, (tm, LANES), 1)
    big = jnp.int32(LANES)
    rmax = lambda a: jnp.max(a, axis=-1, keepdims=True)
    rmin = lambda a: jnp.min(a, axis=-1, keepdims=True)
    rsum = lambda a: jnp.sum(a, axis=-1, keepdims=True)
    is_grp = lane < N_EXPERT_GROUPS
    m_g = rmax(jnp.where(is_grp, logits, -jnp.inf))
    p_g = 1.0 / rsum(jnp.where(is_grp, jnp.exp(logits - m_g), 0.0))
    g_sel = rmin(jnp.where(is_grp & (logits == m_g), lane, big))
    lo = N_EXPERT_GROUPS + EXPERTS_PER_GROUP * g_sel
    in_grp = (lane >= lo) & (lane < lo + EXPERTS_PER_GROUP)
    le = jnp.where(in_grp, logits, -jnp.inf)
    m_1 = rmax(le)
    i_1 = rmin(jnp.where(le == m_1, lane, big))
    le2 = jnp.where(lane == i_1, -jnp.inf, le)
    m_2 = rmax(le2)
    i_2 = rmin(jnp.where(le2 == m_2, lane, big))
    e_2 = jnp.exp(m_2 - m_1)
    w_1 = p_g / (1.0 + e_2)
    w_2 = p_g * e_2 / (1.0 + e_2)
    eid_1 = i_1 - N_EXPERT_GROUPS
    eid_2 = i_2 - N_EXPERT_GROUPS

    oh1 = lane == eid_1
    oh2 = lane == eid_2
    onehot = jnp.where(oh1, 1.0, 0.0) + jnp.where(oh2, 1.0, 0.0)
    r_i = lax.broadcasted_iota(jnp.int32, (tm, tm), 0)
    c_i = lax.broadcasted_iota(jnp.int32, (tm, tm), 1)
    earlier = jnp.where(c_i < r_i, 1.0, 0.0).astype(bf)
    before = jnp.dot(earlier, onehot.astype(bf), preferred_element_type=f32) + run_ref[...]
    rank_1 = rsum(jnp.where(oh1, before, 0.0))
    rank_2 = rsum(jnp.where(oh2, before, 0.0))
    run_ref[...] = run_ref[...] + jnp.sum(onehot, axis=0, keepdims=True)
    counts_ref[...] = run_ref[...]

    meta = jnp.zeros((tm, LANES), f32)
    for off, val in ((_META_EID, eid_1.astype(f32)), (_META_EID + 1, eid_2.astype(f32)),
                     (_META_W, w_1), (_META_W + 1, w_2), (_META_RANK, rank_1), (_META_RANK + 1, rank_2)):
        meta = jnp.where(lane == off, val, meta)
    meta_ref[...] = meta


def _mix(attn, p, x2d, ada3, w_pool, pool_scale, w_out, norm2_g, w_grp, w_exp, B, S):
    T, D = x2d.shape
    tm = min(256, S)
    tiles_per_seq = S // tm
    halo_blocks = tm // POOL_HALO
    bf = jnp.bfloat16
    w_router = jnp.concatenate(
        [w_grp, w_exp, jnp.zeros((D, LANES - N_EXPERT_GROUPS - N_EXPERTS), w_grp.dtype)], axis=1)
    row = lambda w: pl.BlockSpec((tm, w), lambda i: (i, 0))
    const = lambda shape: pl.BlockSpec(shape, lambda i: (0,) * len(shape))
    once = lambda shape: pl.BlockSpec(shape, lambda i: (0,) * len(shape), pipeline_mode=pl.Buffered(1))
    kern = functools.partial(_mix_kernel, tm=tm, tiles_per_seq=tiles_per_seq)
    return pl.pallas_call(
        kern,
        grid=(T // tm,),
        in_specs=[row(ATTN_WIDTH), row(POOL_WIDTH),
                  pl.BlockSpec((POOL_HALO, POOL_WIDTH), lambda i: (jnp.maximum(i * halo_blocks - 1, 0), 0)),
                  row(D),
                  pl.BlockSpec((1, 6, D), lambda i: (i // tiles_per_seq, 0, 0)),
                  once((POOL_GROUPS, POOL_GROUP_DIM, POOL_GROUP_DIM)), const((1, POOL_WIDTH)),
                  once((ATTN_WIDTH + POOL_WIDTH, D)), const((1, D)), once((D, LANES))],
        out_specs=[row(D), row(D), row(LANES), const((1, LANES))],
        out_shape=[jax.ShapeDtypeStruct((T, D), jnp.float32),
                   jax.ShapeDtypeStruct((T, D), jnp.float32),
                   jax.ShapeDtypeStruct((T, LANES), jnp.float32),
                   jax.ShapeDtypeStruct((1, LANES), jnp.float32)],
        scratch_shapes=[pltpu.VMEM((POOL_HALO + tm, POOL_WIDTH), jnp.float32),
                        pltpu.VMEM((1, LANES), jnp.float32),
                        pltpu.VMEM((2, D, LANES), bf)],
        compiler_params=_cparams(("arbitrary",)),
        name="mix",
    )(attn, p, p, x2d, ada3, w_pool.astype(bf), pool_scale.reshape(1, POOL_WIDTH), w_out.astype(bf),
      norm2_g.reshape(1, D), w_router)


def _dispatch_kernel(pad_start_ref, pad_len_ref, n_used_ref, pos_ref, h2_ref, xs_ref, zeros_ref, sem, *, rt, n_tiles):
    i = pl.program_id(0)

    @pl.when(i == 0)
    def _():
        zeros_ref[...] = jnp.zeros(zeros_ref.shape, zeros_ref.dtype)

        def zero_row(r):
            return pltpu.make_async_copy(zeros_ref.at[pl.ds(0, 1)], xs_ref.at[pl.ds(r, 1)], sem)

        def zero_tile(t):
            start = pl.multiple_of(t * EXPERT_TILE, EXPERT_TILE)
            return pltpu.make_async_copy(zeros_ref, xs_ref.at[pl.ds(start, EXPERT_TILE)], sem)

        def fill(e, carry):
            start = pad_start_ref[e]
            n = pad_len_ref[e]
            lax.fori_loop(0, n, lambda r, c: (zero_row(start + r).start(), c)[1], 0)
            lax.fori_loop(0, n, lambda r, c: (zero_row(0).wait(), c)[1], 0)
            return carry

        lax.fori_loop(0, N_EXPERTS, fill, 0)
        first_unused = n_used_ref[0]
        lax.fori_loop(first_unused, n_tiles, lambda t, c: (zero_tile(t).start(), c)[1], 0)
        lax.fori_loop(first_unused, n_tiles, lambda t, c: (zero_tile(0).wait(), c)[1], 0)

    def row_copy(t, slot):
        return pltpu.make_async_copy(h2_ref.at[pl.ds(t, 1)], xs_ref.at[pl.ds(pos_ref[0, 0, 2 * t + slot], 1)], sem)

    def issue(t, carry):
        row_copy(t, 0).start(priority=0)
        row_copy(t, 1).start(priority=1)
        return carry

    def drain(t, carry):
        for _ in range(2):
            pltpu.make_async_copy(h2_ref.at[pl.ds(0, 1)], xs_ref.at[pl.ds(0, 1)], sem).wait()
        return carry

    lax.fori_loop(0, rt, issue, 0)
    lax.fori_loop(0, rt, drain, 0)


def _dispatch(h2, pos, pad_start, pad_len, n_used, n_rows):
    T, D = h2.shape
    rt = min(ROW_TILE, T)
    kern = functools.partial(_dispatch_kernel, rt=rt, n_tiles=n_rows // EXPERT_TILE)
    return pl.pallas_call(
        kern,
        grid_spec=pltpu.PrefetchScalarGridSpec(
            num_scalar_prefetch=3,
            grid=(T // rt,),
            in_specs=[pl.BlockSpec((1, 1, 2 * rt), lambda i, *_: (i, 0, 0), memory_space=pltpu.SMEM),
                      pl.BlockSpec((rt, D), lambda i, *_: (i, 0))],
            out_specs=pl.BlockSpec(memory_space=pl.ANY),
            scratch_shapes=[pltpu.VMEM((EXPERT_TILE, D), h2.dtype), pltpu.SemaphoreType.DMA(())]),
        out_shape=jax.ShapeDtypeStruct((n_rows, D), h2.dtype),
        compiler_params=_cparams(("arbitrary",)),
        name="dispatch",
    )(pad_start, pad_len, n_used, pos.reshape(T // rt, 1, 2 * rt), h2)


def _experts_kernel(tile_expert_ref, n_used_ref, xs_ref, w1_ref, w3_ref, w2_ref, ys_ref, w1b_ref, w3b_ref, w2b_ref):
    i = pl.program_id(0)
    bf = jnp.bfloat16

    @pl.when((i == 0) | (tile_expert_ref[i] != tile_expert_ref[jnp.maximum(i - 1, 0)]))
    def _():
        w1b_ref[...] = w1_ref[0].astype(bf)
        w3b_ref[...] = w3_ref[0].astype(bf)
        w2b_ref[...] = w2_ref[0].astype(bf)

    @pl.when(i < n_used_ref[0])
    def _():
        xb = xs_ref[...].astype(bf)
        a1 = jnp.dot(xb, w1b_ref[...], preferred_element_type=jnp.float32)
        a3 = jnp.dot(xb, w3b_ref[...], preferred_element_type=jnp.float32)
        act = (a1 * (1.0 / (1.0 + jnp.exp(-a1))) * a3).astype(bf)
        ys_ref[...] = jnp.dot(act, w2b_ref[...], preferred_element_type=jnp.float32)

    @pl.when(i >= n_used_ref[0])
    def _():
        ys_ref[...] = jnp.zeros(ys_ref.shape, ys_ref.dtype)


def _experts(xs, w1, w3, w2, tile_expert, n_used):
    n_rows, D = xs.shape
    n_tiles = n_rows // EXPERT_TILE
    bf = jnp.bfloat16
    return pl.pallas_call(
        _experts_kernel,
        grid_spec=pltpu.PrefetchScalarGridSpec(
            num_scalar_prefetch=2,
            grid=(n_tiles,),
            in_specs=[pl.BlockSpec((EXPERT_TILE, D), lambda i, te, nu: (jnp.minimum(i, nu[0] - 1), 0)),
                      pl.BlockSpec((1, D, D_EXPERT), lambda i, te, nu: (te[i], 0, 0)),
                      pl.BlockSpec((1, D, D_EXPERT), lambda i, te, nu: (te[i], 0, 0)),
                      pl.BlockSpec((1, D_EXPERT, D), lambda i, te, nu: (te[i], 0, 0))],
            out_specs=pl.BlockSpec((EXPERT_TILE, D), lambda i, te, nu: (i, 0)),
            scratch_shapes=[pltpu.VMEM((D, D_EXPERT), bf), pltpu.VMEM((D, D_EXPERT), bf),
                            pltpu.VMEM((D_EXPERT, D), bf)]),
        out_shape=jax.ShapeDtypeStruct((n_rows, D), jnp.float32),
        compiler_params=_cparams(("arbitrary",)),
        name="experts",
    )(tile_expert, n_used, xs, w1, w3, w2)


def _combine_kernel(pos_ref, pos_next_ref, ys_ref, x1_ref, meta_ref, ada_ref, o_ref, rows_ref, sems, *, rt):
    i = pl.program_id(0)
    buf = i % 2

    def gather(p_ref, b):
        def issue(t, carry):
            for slot in range(2):
                pltpu.make_async_copy(ys_ref.at[pl.ds(p_ref[0, 0, 2 * t + slot], 1)],
                                      rows_ref.at[b, slot, pl.ds(t, 1)], sems.at[b]).start(priority=slot)
            return carry
        lax.fori_loop(0, rt, issue, 0)

    @pl.when(i == 0)
    def _():
        gather(pos_ref, 0)

    @pl.when(i + 1 < pl.num_programs(0))
    def _():
        gather(pos_next_ref, 1 - buf)

    def drain(t, carry):
        for slot in range(2):
            pltpu.make_async_copy(ys_ref.at[pl.ds(0, 1)], rows_ref.at[buf, slot, pl.ds(0, 1)], sems.at[buf]).wait()
        return carry

    lax.fori_loop(0, rt, drain, 0)
    meta = meta_ref[...]
    y = meta[:, _META_W:_META_W + 1] * rows_ref[buf, 0] + meta[:, _META_W + 1:_META_W + 2] * rows_ref[buf, 1]
    o_ref[...] = x1_ref[...] + ada_ref[0, 5:6, :] * y


def _combine(ys, pos, x1, meta, ada3, S):
    T, D = x1.shape
    rt = min(ROW_TILE, S)
    tiles_per_seq = S // rt
    kern = functools.partial(_combine_kernel, rt=rt)
    n_steps = T // rt
    pos_blocks = pos.reshape(n_steps, 1, 2 * rt)
    return pl.pallas_call(
        kern,
        grid=(n_steps,),
        in_specs=[pl.BlockSpec((1, 1, 2 * rt), lambda i: (i, 0, 0), memory_space=pltpu.SMEM),
                  pl.BlockSpec((1, 1, 2 * rt), lambda i: (jnp.minimum(i + 1, n_steps - 1), 0, 0),
                               memory_space=pltpu.SMEM),
                  pl.BlockSpec(memory_space=pl.ANY),
                  pl.BlockSpec((rt, D), lambda i: (i, 0)),
                  pl.BlockSpec((rt, LANES), lambda i: (i, 0)),
                  pl.BlockSpec((1, 6, D), lambda i: (i // tiles_per_seq, 0, 0))],
        out_specs=pl.BlockSpec((rt, D), lambda i: (i, 0)),
        out_shape=jax.ShapeDtypeStruct((T, D), jnp.float32),
        scratch_shapes=[pltpu.VMEM((2, 2, rt, D), jnp.float32), pltpu.SemaphoreType.DMA((2,))],
        compiler_params=_cparams(("arbitrary",)),
        name="combine",
    )(pos_blocks, pos_blocks, ys, x1, meta, ada3)


def _routing_plan(meta, counts, T):
    i32 = jnp.int32
    eid = meta[:, _META_EID:_META_EID + 2].reshape(-1).astype(i32)
    rank = meta[:, _META_RANK:_META_RANK + 2].reshape(-1).astype(i32)
    cnt = counts[0, :N_EXPERTS].astype(i32)
    padded = (cnt + EXPERT_TILE - 1) // EXPERT_TILE * EXPERT_TILE
    ends = jnp.cumsum(padded)
    starts = ends - padded
    pos = starts[eid] + rank
    n_rows = 2 * T + N_EXPERTS * EXPERT_TILE
    n_tiles = n_rows // EXPERT_TILE
    n_used = (ends[-1] // EXPERT_TILE).astype(i32)
    tile_start = jnp.minimum(jnp.arange(n_tiles, dtype=i32), n_used - 1) * EXPERT_TILE
    tile_expert = jnp.sum((ends[None, :] <= tile_start[:, None]).astype(i32), axis=1)
    return pos, starts + cnt, padded - cnt, tile_expert, n_used.reshape(1), n_rows


def kernel(x, c, w_ada, b_ada, norm1_g, w_in, q_norm_g, k_norm_g, w_pool, pool_scale,
           w_out, norm2_g, w_grp, w_exp, w1, w3, w2):
    B, S, D = x.shape
    T = B * S
    x2d = x.reshape(T, D)
    for l in range(w_ada.shape[0]):
        ada3 = _ada(c, w_ada[l], b_ada[l]).reshape(B, 6, D)
        q, kt, v, qi, kia, kib, wi, p = _inproj(x2d, ada3, norm1_g[l], w_in[l], q_norm_g[l], k_norm_g[l], B, S)
        attn = _dsa(q, qi, wi, kia, kib, kt, v, B, S)
        x1, h2, meta, counts = _mix(attn, p, x2d, ada3, w_pool[l], pool_scale[l], w_out[l], norm2_g[l],
                                    w_grp[l], w_exp[l], B, S)
        pos, pad_start, pad_len, tile_expert, n_used, n_rows = _routing_plan(meta, counts, T)
        xs = _dispatch(h2, pos, pad_start, pad_len, n_used, n_rows)
        ys = _experts(xs, w1[l], w3[l], w2[l], tile_expert, n_used)
        x2d = _combine(ys, pos, x1, meta, ada3, S)
    return x2d.reshape(B, S, D)
```

```python
import functools

import numpy as np
import jax
import jax.numpy as jnp
from jax import lax
from jax.experimental import pallas as pl
from jax.experimental.pallas import tpu as pltpu

N_HEADS = 8
N_KV_HEADS = 2
HEAD_DIM = 128
ATTN_WIDTH = N_HEADS * HEAD_DIM
KV_WIDTH = N_KV_HEADS * HEAD_DIM
ROPE_THETA = 500000.0
ROPE_FRACTION = 4
IDX_HEADS = 16
IDX_DIM = 64
TOPK_MAX = 256
POOL_GROUPS = 4
POOL_WINDOWS = (2, 4, 8, 16)
POOL_GROUP_DIM = 256
POOL_WIDTH = POOL_GROUPS * POOL_GROUP_DIM
N_EXPERT_GROUPS = 4
EXPERTS_PER_GROUP = 8
N_EXPERTS = N_EXPERT_GROUPS * EXPERTS_PER_GROUP
D_EXPERT = 512
NORM_EPS = 1e-6

LANES = 128
VMEM_LIMIT_BYTES = 56 * 1024 * 1024

KEY_TILE = 512
Q_TILE = 128
POOL_HALO = 16
EXPERT_TILE = 256
ROW_TILE = 512

_C_Q = 0
_C_K = _C_Q + ATTN_WIDTH
_C_V = _C_K + KV_WIDTH
_C_QI = _C_V + KV_WIDTH
_C_KI = _C_QI + IDX_HEADS * IDX_DIM
_C_WI = _C_KI + LANES
_C_P = _C_WI + LANES
_C_END = _C_P + POOL_WIDTH

_NEG = -1e30
_LOG2_E = 1.4426950408889634
_INT_MIN = -2 ** 31
_KEY_LOWEST_FINITE = -2 ** 31 + 0x00800000
_BISECT_FIXED_PASSES = 22


def _cparams(semantics):
    return pltpu.CompilerParams(dimension_semantics=semantics, vmem_limit_bytes=VMEM_LIMIT_BYTES)


def _ada_kernel(c_ref, w_ref, b_ref, o_ref):
    c = c_ref[...]
    s = c * (1.0 / (1.0 + jnp.exp(-c)))
    o_ref[...] = jnp.dot(s, w_ref[...], preferred_element_type=jnp.float32,
                         precision=lax.Precision.HIGHEST) + b_ref[...]


def _ada(c, w_ada, b_ada):
    B, D = c.shape
    N = w_ada.shape[1]
    tn = 1024
    rows = 8
    c_pad = jnp.zeros((rows, D), jnp.float32).at[:B].set(c)
    out = pl.pallas_call(
        _ada_kernel,
        grid=(N // tn,),
        in_specs=[pl.BlockSpec((rows, D), lambda j: (0, 0)),
                  pl.BlockSpec((D, tn), lambda j: (0, j)),
                  pl.BlockSpec((1, tn), lambda j: (0, j))],
        out_specs=pl.BlockSpec((rows, tn), lambda j: (0, j)),
        out_shape=jax.ShapeDtypeStruct((rows, N), jnp.float32),
        compiler_params=_cparams(("arbitrary",)),
        name="ada",
    )(c_pad, w_ada, b_ada.reshape(1, N))
    return out[:B]


def _rope_tables(S, head_dim):
    rd = head_dim // ROPE_FRACTION
    half = rd // 2
    pos = jnp.arange(S, dtype=jnp.float32)
    inv = jnp.float32(ROPE_THETA) ** (-(jnp.arange(half, dtype=jnp.float32) * 2.0) / rd)
    ang = pos[:, None] * inv[None, :]
    cos, sin = jnp.cos(ang), jnp.sin(ang)
    lane = np.arange(LANES) % head_dim
    fidx = np.where(lane < half, lane, lane - half) % half
    in_lo = jnp.asarray(lane < half)[None, :]
    in_hi = jnp.asarray((lane >= half) & (lane < rd))[None, :]
    cos_l, sin_l = cos[:, fidx], sin[:, fidx]
    c_tab = jnp.where(in_lo | in_hi, cos_l, 1.0)
    s1_tab = jnp.where(in_lo, -sin_l, 0.0)
    s2_tab = jnp.where(in_hi, sin_l, 0.0)
    return c_tab, s1_tab, s2_tab, half


def _rope(x, c_tab, s1_tab, s2_tab, half):
    return (x * c_tab + pltpu.roll(x, LANES - half, 1) * s1_tab + pltpu.roll(x, half, 1) * s2_tab)


def _inproj_kernel(x_ref, ada_ref, g1_ref, w_ref, qg_ref, kg_ref,
                   cq_ref, s1q_ref, s2q_ref, ci_ref, s1i_ref, s2i_ref,
                   q_ref, kt_ref, v_ref, qi_ref, kia_ref, kib_ref, wi_ref, p_ref,
                   *, half_qk, half_idx, n_chunks):
    x = x_ref[...]
    ms = jnp.mean(x * x, axis=-1, keepdims=True)
    xn = x * lax.rsqrt(ms + NORM_EPS) * g1_ref[...]
    h = (xn * (1.0 + ada_ref[0, 1:2, :]) + ada_ref[0, 0:1, :]).astype(jnp.bfloat16)

    def proj(lo, hi):
        return jnp.dot(h, w_ref[:, lo:hi], preferred_element_type=jnp.float32)

    cq, s1q, s2q = cq_ref[...], s1q_ref[...], s2q_ref[...]
    ci, s1i, s2i = ci_ref[...], s1i_ref[...], s2i_ref[...]

    def qk_head(slab, gain):
        m = jnp.mean(slab * slab, axis=-1, keepdims=True)
        y = slab * lax.rsqrt(m + NORM_EPS) * gain
        return _rope(y, cq, s1q, s2q, half_qk)

    attn_scale = HEAD_DIM ** -0.5 * _LOG2_E
    q = proj(_C_Q, _C_K)
    for hd in range(N_HEADS):
        sl = slice(hd * HEAD_DIM, (hd + 1) * HEAD_DIM)
        q_ref[:, sl] = (qk_head(q[:, sl], qg_ref[...]) * attn_scale).astype(jnp.bfloat16)

    k = proj(_C_K, _C_V)
    for g in range(N_KV_HEADS):
        kt = qk_head(k[:, g * HEAD_DIM:(g + 1) * HEAD_DIM], kg_ref[...]).T.astype(jnp.bfloat16)
        for c in range(n_chunks):
            kt_ref[0, g, c] = kt[:, c * KEY_TILE:(c + 1) * KEY_TILE]

    v_ref[...] = proj(_C_V, _C_QI).astype(jnp.bfloat16)

    qi = proj(_C_QI, _C_KI)
    for j in range(IDX_HEADS * IDX_DIM // LANES):
        sl = slice(j * LANES, (j + 1) * LANES)
        qi_ref[:, sl] = _rope(qi[:, sl], ci, s1i, s2i, half_idx).astype(jnp.bfloat16)

    ki = _rope(proj(_C_KI, _C_WI), ci, s1i, s2i, half_idx).T
    ki_swapped = jnp.concatenate([ki[IDX_DIM:], ki[:IDX_DIM]], axis=0)
    for c in range(n_chunks):
        kia_ref[0, c] = ki[:, c * KEY_TILE:(c + 1) * KEY_TILE].astype(jnp.bfloat16)
        kib_ref[0, c] = ki_swapped[:, c * KEY_TILE:(c + 1) * KEY_TILE].astype(jnp.bfloat16)

    idx_scale = (IDX_DIM ** -0.5) * (IDX_HEADS ** -0.5)
    wi_ref[...] = proj(_C_WI, _C_P) * idx_scale
    p_ref[...] = proj(_C_P, _C_END)


def _inproj(x2d, ada3, norm1_g, w_in, q_norm_g, k_norm_g, B, S):
    T, D = x2d.shape
    tm = min(512, S)
    n_chunks = tm // KEY_TILE
    tiles_per_seq = S // tm
    bf = jnp.bfloat16

    offs = np.cumsum((ATTN_WIDTH, KV_WIDTH, KV_WIDTH, IDX_HEADS * IDX_DIM, IDX_DIM, IDX_HEADS))
    o_q, o_k, o_v, o_qi, o_ki, o_wi = [int(o) for o in offs]
    zeros = lambda n: jnp.zeros((D, n), w_in.dtype)
    w_pad = jnp.concatenate([
        w_in[:, :o_qi],
        w_in[:, o_qi:o_ki], zeros(LANES - IDX_DIM),
        w_in[:, o_ki:o_wi], zeros(LANES - IDX_HEADS),
        w_in[:, o_wi:],
    ], axis=1).astype(bf)
    assert w_pad.shape[1] == _C_END

    cq, s1q, s2q, half_qk = _rope_tables(S, HEAD_DIM)
    ci, s1i, s2i, half_idx = _rope_tables(S, IDX_DIM)

    row = lambda w: pl.BlockSpec((tm, w), lambda i: (i, 0))
    tab = pl.BlockSpec((tm, LANES), lambda i: (i % tiles_per_seq, 0))
    const = lambda shape: pl.BlockSpec(shape, lambda i: (0,) * len(shape))
    kern = functools.partial(_inproj_kernel, half_qk=half_qk, half_idx=half_idx, n_chunks=n_chunks)
    return pl.pallas_call(
        kern,
        grid=(T // tm,),
        in_specs=[row(D),
                  pl.BlockSpec((1, 6, D), lambda i: (i // tiles_per_seq, 0, 0)),
                  const((1, D)),
                  pl.BlockSpec((D, _C_END), lambda i: (0, 0), pipeline_mode=pl.Buffered(1)),
                  const((1, HEAD_DIM)), const((1, HEAD_DIM)),
                  tab, tab, tab, tab, tab, tab],
        out_specs=[row(ATTN_WIDTH),
                   pl.BlockSpec((1, N_KV_HEADS, n_chunks, HEAD_DIM, KEY_TILE),
                                lambda i: (i // tiles_per_seq, 0, i % tiles_per_seq, 0, 0)),
                   row(KV_WIDTH),
                   row(IDX_HEADS * IDX_DIM),
                   pl.BlockSpec((1, n_chunks, LANES, KEY_TILE),
                                lambda i: (i // tiles_per_seq, i % tiles_per_seq, 0, 0)),
                   pl.BlockSpec((1, n_chunks, LANES, KEY_TILE),
                                lambda i: (i // tiles_per_seq, i % tiles_per_seq, 0, 0)),
                   row(LANES),
                   row(POOL_WIDTH)],
        out_shape=[jax.ShapeDtypeStruct((T, ATTN_WIDTH), bf),
                   jax.ShapeDtypeStruct((B, N_KV_HEADS, S // KEY_TILE, HEAD_DIM, KEY_TILE), bf),
                   jax.ShapeDtypeStruct((T, KV_WIDTH), bf),
                   jax.ShapeDtypeStruct((T, IDX_HEADS * IDX_DIM), bf),
                   jax.ShapeDtypeStruct((B, S // KEY_TILE, LANES, KEY_TILE), bf),
                   jax.ShapeDtypeStruct((B, S // KEY_TILE, LANES, KEY_TILE), bf),
                   jax.ShapeDtypeStruct((T, LANES), jnp.float32),
                   jax.ShapeDtypeStruct((T, POOL_WIDTH), jnp.float32)],
        compiler_params=_cparams(("arbitrary",)),
        name="inproj",
    )(x2d, ada3, norm1_g.reshape(1, D), w_pad, q_norm_g.reshape(1, HEAD_DIM), k_norm_g.reshape(1, HEAD_DIM),
      cq, s1q, s2q, ci, s1i, s2i)


def _dsa_kernel(q_ref, qi_ref, wi_ref, kia_ref, kib_ref, kt_ref, v_ref, o_ref,
                keys_ref, keyst_ref, wb_ref, m_ref, acc_ref, s_ref, *, tq, n_sel, idx_bits):
    qb = pl.program_id(1)
    n_tiles = (qb * tq + tq + KEY_TILE - 1) // KEY_TILE
    lane_chunks = KEY_TILE // LANES
    rep = N_HEADS // N_KV_HEADS
    f32 = jnp.float32

    qi = qi_ref[...]
    n_slabs = IDX_HEADS * IDX_DIM // LANES
    lhs = jnp.concatenate([qi[:, j * LANES:(j + 1) * LANES] for j in range(n_slabs)], axis=0)
    w = wi_ref[...]
    for hd in range(IDX_HEADS):
        wb_ref[hd] = jnp.broadcast_to(w[:, hd:hd + 1], (tq, LANES))
    t_pos = qb * tq + lax.broadcasted_iota(jnp.int32, (tq, KEY_TILE), 0)
    k_off = lax.broadcasted_iota(jnp.int32, (tq, KEY_TILE), 1)

    def sweep_tiles(body, init=0, group=2):
        shift = group.bit_length() - 1
        n_groups = lax.shift_right_logical(n_tiles, shift)

        def grouped(i, carry):
            for u in range(group):
                carry = body(group * i + u, carry)
            return carry

        carry = lax.fori_loop(0, n_groups, grouped, init)
        return lax.fori_loop(group * n_groups, n_tiles, body, carry)

    def score_tile(j, carry):
        za = jnp.dot(lhs, kia_ref[0, j], preferred_element_type=f32)
        zb = jnp.dot(lhs, kib_ref[0, j], preferred_element_type=f32)
        acc = jnp.zeros((tq, KEY_TILE), f32)
        for s in range(n_slabs):
            wa = jnp.concatenate([wb_ref[2 * s]] * lane_chunks, axis=1)
            wo = jnp.concatenate([wb_ref[2 * s + 1]] * lane_chunks, axis=1)
            acc = acc + wa * jnp.maximum(za[s * tq:(s + 1) * tq], 0.0)
            acc = acc + wo * jnp.maximum(zb[s * tq:(s + 1) * tq], 0.0)
        score = jnp.where(j * KEY_TILE + k_off <= t_pos, acc, -jnp.inf)
        bits = lax.bitcast_convert_type(score, jnp.int32)
        key = bits ^ ((bits >> 31) & 0x7FFFFFFF)
        keys_ref[j] = key
        for cc in range(lane_chunks):
            keyst_ref[j, cc * LANES:(cc + 1) * LANES, :] = key[:, cc * LANES:(cc + 1) * LANES].T
        return carry

    sweep_tiles(score_tile)

    def rows_to_lanes_sum(hit):
        return jnp.sum(hit.reshape(KEY_TILE // 8, 8, tq), axis=0)

    def count_ge(cand):
        def body(j, cnt):
            return cnt + rows_to_lanes_sum(jnp.where(keyst_ref[j] >= cand, 1, 0))
        cnt = sweep_tiles(body, jnp.zeros((8, tq), jnp.int32), group=4)
        return jnp.sum(cnt, axis=0, keepdims=True)

    def bisect(i, state):
        thr, at_thr = state
        cand = thr + lax.shift_left(jnp.int32(1), 31 - i)
        cnt = count_ge(cand)
        take = cnt >= n_sel
        return jnp.where(take, cand, thr), jnp.where(take, cnt, at_thr)

    state = (jnp.full((1, tq), _INT_MIN, jnp.int32), jnp.full((1, tq), 2 ** 31 - 1, jnp.int32))
    state = lax.fori_loop(0, _BISECT_FIXED_PASSES, bisect, state)

    def unsettled(carry):
        i, _, at_thr = carry
        return (i < 32) & (jnp.max(jnp.where(at_thr == n_sel, 0, 1)) > 0)

    def two_more_bits(carry):
        i, thr, at_thr = carry
        thr, at_thr = bisect(i + 1, bisect(i, (thr, at_thr)))
        return i + 2, thr, at_thr

    _, thr_t, at_thr_t = lax.while_loop(unsettled, two_more_bits, (jnp.int32(_BISECT_FIXED_PASSES),) + state)

    def per_query_rows(vec_t):
        rows = jnp.broadcast_to(vec_t, (tq, tq)).T
        return jnp.concatenate([rows] * lane_chunks, axis=1)

    thr_wide = per_query_rows(thr_t)

    tied = (at_thr_t > n_sel) & (thr_t >= _KEY_LOWEST_FINITE)
    any_tied = jnp.max(jnp.where(tied, 1, 0)) > 0

    @pl.when(any_tied)
    def _():
        need = n_sel - count_ge(thr_t + 1)
        row_idx = lax.broadcasted_iota(jnp.int32, (KEY_TILE, tq), 0)

        def tied_before(limit):
            def body(j, cnt):
                hit = jnp.where(j * KEY_TILE + row_idx < limit, 1, 0)
                return cnt + rows_to_lanes_sum(jnp.where(keyst_ref[j] == thr_t, hit, 0))
            cnt = lax.fori_loop(0, n_tiles, body, jnp.zeros((8, tq), jnp.int32))
            return jnp.sum(cnt, axis=0, keepdims=True)

        def bisect_index(i, last):
            cand = last + lax.shift_left(jnp.int32(1), idx_bits - 1 - i)
            return jnp.where(tied_before(cand) < need, cand, last)

        last_t = lax.fori_loop(0, idx_bits, bisect_index, jnp.zeros((1, tq), jnp.int32))
        last_wide = per_query_rows(last_t)

        def demote(j, carry):
            kk = keys_ref[j]
            drop = jnp.where(j * KEY_TILE + k_off > last_wide, thr_wide - 1, kk)
            keys_ref[j] = jnp.where(kk == thr_wide, drop, kk)
            return carry

        lax.fori_loop(0, n_tiles, demote, 0)

    thr_wide = jnp.maximum(thr_wide, _KEY_LOWEST_FINITE)

    q = q_ref[...]
    ones_col = jnp.ones((KEY_TILE, HEAD_DIM), jnp.bfloat16)
    for g in range(N_KV_HEADS):
        qg = jnp.concatenate([q[:, (g * rep + r) * HEAD_DIM:(g * rep + r + 1) * HEAD_DIM] for r in range(rep)],
                             axis=0)
        m_ref[...] = jnp.full(m_ref.shape, _NEG, f32)
        acc_ref[...] = jnp.zeros(acc_ref.shape, f32)

        def scores(j, carry, qg=qg, g=g):
            bias = jnp.where(keys_ref[j] >= thr_wide, 0.0, _NEG)
            s = jnp.dot(qg, kt_ref[0, g, j], preferred_element_type=f32) + jnp.concatenate([bias] * rep, axis=0)
            s_ref[j] = s
            m_part = m_ref[...]
            for cc in range(lane_chunks):
                m_part = jnp.maximum(m_part, s[:, cc * LANES:(cc + 1) * LANES])
            m_ref[...] = m_part
            return carry

        sweep_tiles(scores, group=4)
        m_ref[...] = jnp.broadcast_to(jnp.max(m_ref[...], axis=-1, keepdims=True), m_ref.shape)

        def values(j, carry, g=g):
            m_row = m_ref[...]
            p = jnp.exp2(s_ref[j] - jnp.concatenate([m_row] * lane_chunks, axis=1)).astype(jnp.bfloat16)
            start = pl.multiple_of(j * KEY_TILE, KEY_TILE)
            vt = jnp.concatenate([v_ref[pl.ds(start, KEY_TILE), g * HEAD_DIM:(g + 1) * HEAD_DIM], ones_col],
                                 axis=1)
            acc_ref[...] += jnp.dot(p, vt, preferred_element_type=f32)
            return carry

        sweep_tiles(values, group=4)
        acc = acc_ref[...]
        o = acc[:, :HEAD_DIM] / acc[:, HEAD_DIM:]
        for r in range(rep):
            hd = g * rep + r
            o_ref[:, hd * HEAD_DIM:(hd + 1) * HEAD_DIM] = o[r * tq:(r + 1) * tq].astype(o_ref.dtype)


def _dsa(q, qi, wi, kia, kib, kt, v, B, S):
    tq = min(Q_TILE, S)
    n_sel = min(TOPK_MAX, S // 4)
    nq = S // tq
    n_kt = S // KEY_TILE
    rep = N_HEADS // N_KV_HEADS
    row = lambda w: pl.BlockSpec((tq, w), lambda b, i: (b * nq + i, 0))
    once = pl.Buffered(1)
    kern = functools.partial(_dsa_kernel, tq=tq, n_sel=n_sel, idx_bits=max(1, (S - 1).bit_length()))
    return pl.pallas_call(
        kern,
        grid=(B, nq),
        in_specs=[row(ATTN_WIDTH), row(IDX_HEADS * IDX_DIM), row(LANES),
                  pl.BlockSpec((1, n_kt, LANES, KEY_TILE), lambda b, i: (b, 0, 0, 0), pipeline_mode=once),
                  pl.BlockSpec((1, n_kt, LANES, KEY_TILE), lambda b, i: (b, 0, 0, 0), pipeline_mode=once),
                  pl.BlockSpec((1, N_KV_HEADS, n_kt, HEAD_DIM, KEY_TILE), lambda b, i: (b, 0, 0, 0, 0),
                               pipeline_mode=once),
                  pl.BlockSpec((S, KV_WIDTH), lambda b, i: (b, 0), pipeline_mode=once)],
        out_specs=row(ATTN_WIDTH),
        out_shape=jax.ShapeDtypeStruct((B * S, ATTN_WIDTH), jnp.bfloat16),
        scratch_shapes=[pltpu.VMEM((n_kt, tq, KEY_TILE), jnp.int32),
                        pltpu.VMEM((n_kt, KEY_TILE, tq), jnp.int32),
                        pltpu.VMEM((IDX_HEADS, tq, LANES), jnp.float32),
                        pltpu.VMEM((rep * tq, LANES), jnp.float32),
                        pltpu.VMEM((rep * tq, 2 * HEAD_DIM), jnp.float32),
                        pltpu.VMEM((n_kt, rep * tq, KEY_TILE), jnp.float32)],
        compiler_params=_cparams(("arbitrary", "arbitrary")),
        name="dsa",
    )(q, qi, wi, kia, kib, kt, v)


_META_EID, _META_W, _META_RANK = 0, 2, 4


def _mix_kernel(attn_ref, p_ref, halo_ref, x_ref, ada_ref, wpool_ref, pscale_ref, wout_ref, g2_ref, wr_ref,
                x1_ref, h2_ref, meta_ref, counts_ref, pbuf_ref, run_ref, wrs_ref, *, tm, tiles_per_seq):
    i = pl.program_id(0)
    f32 = jnp.float32
    bf = jnp.bfloat16
    seq_tile = i % tiles_per_seq

    @pl.when(i == 0)
    def _():
        run_ref[...] = jnp.zeros(run_ref.shape, f32)
        wr = wr_ref[...]
        wr_hi = wr.astype(bf)
        wrs_ref[0] = wr_hi
        wrs_ref[1] = (wr - wr_hi.astype(f32)).astype(bf)

    p = p_ref[...]
    pbuf_ref[0:POOL_HALO, :] = jnp.where(seq_tile == 0, 0.0, halo_ref[...])
    pbuf_ref[POOL_HALO:POOL_HALO + tm, :] = p
    t_pos = seq_tile * tm + lax.broadcasted_iota(jnp.int32, (tm, 1), 0)
    pieces = [attn_ref[...]]
    for g, win in enumerate(POOL_WINDOWS):
        cs = slice(g * POOL_GROUP_DIM, (g + 1) * POOL_GROUP_DIM)
        tot = p[:, cs]
        for back in range(1, win):
            tot = tot + pbuf_ref[POOL_HALO - back:POOL_HALO - back + tm, cs]
        count = jnp.minimum(t_pos + 1, win).astype(f32)
        mixed = tot / count - p[:, cs]
        y = jnp.dot(mixed.astype(bf), wpool_ref[g], preferred_element_type=f32) * pscale_ref[:, cs]
        pieces.append(y.astype(bf))
    mix = jnp.dot(jnp.concatenate(pieces, axis=1), wout_ref[...], preferred_element_type=f32)
    x1 = x_ref[...] + ada_ref[0, 2:3, :] * mix
    x1_ref[...] = x1

    ms = jnp.mean(x1 * x1, axis=-1, keepdims=True)
    h2 = (x1 * lax.rsqrt(ms + NORM_EPS) * g2_ref[...]) * (1.0 + ada_ref[0, 4:5, :]) + ada_ref[0, 3:4, :]
    h2_ref[...] = h2

    h2_hi = h2.astype(bf)
    h2_lo = (h2 - h2_hi.astype(f32)).astype(bf)
    logits = (jnp.dot(h2_hi, wrs_ref[0], preferred_element_type=f32)
              + jnp.dot(h2_lo, wrs_ref[0], preferred_element_type=f32)
              + jnp.dot(h2_hi, wrs_ref[1], preferred_element_type=f32))
    lane = lax.broadcasted_iota(jnp.int32, (tm, LANES), 1)
    big = jnp.int32(LANES)
    rmax = lambda a: jnp.max(a, axis=-1, keepdims=True)
    rmin = lambda a: jnp.min(a, axis=-1, keepdims=True)
    rsum = lambda a: jnp.sum(a, axis=-1, keepdims=True)
    is_grp = lane < N_EXPERT_GROUPS
    m_g = rmax(jnp.where(is_grp, logits, -jnp.inf))
    p_g = 1.0 / rsum(jnp.where(is_grp, jnp.exp(logits - m_g), 0.0))
    g_sel = rmin(jnp.where(is_grp & (logits == m_g), lane, big))
    lo = N_EXPERT_GROUPS + EXPERTS_PER_GROUP * g_sel
    in_grp = (lane >= lo) & (lane < lo + EXPERTS_PER_GROUP)
    le = jnp.where(in_grp, logits, -jnp.inf)
    m_1 = rmax(le)
    i_1 = rmin(jnp.where(le == m_1, lane, big))
    le2 = jnp.where(lane == i_1, -jnp.inf, le)
    m_2 = rmax(le2)
    i_2 = rmin(jnp.where(le2 == m_2, lane, big))
    e_2 = jnp.exp(m_2 - m_1)
    w_1 = p_g / (1.0 + e_2)
    w_2 = p_g * e_2 / (1.0 + e_2)
    eid_1 = i_1 - N_EXPERT_GROUPS
    eid_2 = i_2 - N_EXPERT_GROUPS

    oh1 = lane == eid_1
    oh2 = lane == eid_2
    onehot = jnp.where(oh1, 1.0, 0.0) + jnp.where(oh2, 1.0, 0.0)
    r_i = lax.broadcasted_iota(jnp.int32, (tm, tm), 0)
    c_i = lax.broadcasted_iota(jnp.int32, (tm, tm), 1)
    earlier = jnp.where(c_i < r_i, 1.0, 0.0).astype(bf)
    before = jnp.dot(earlier, onehot.astype(bf), preferred_element_type=f32) + run_ref[...]
    rank_1 = rsum(jnp.where(oh1, before, 0.0))
    rank_2 = rsum(jnp.where(oh2, before, 0.0))
    run_ref[...] = run_ref[...] + jnp.sum(onehot, axis=0, keepdims=True)
    counts_ref[...] = run_ref[...]

    meta = jnp.zeros((tm, LANES), f32)
    for off, val in ((_META_EID, eid_1.astype(f32)), (_META_EID + 1, eid_2.astype(f32)),
                     (_META_W, w_1), (_META_W + 1, w_2), (_META_RANK, rank_1), (_META_RANK + 1, rank_2)):
        meta = jnp.where(lane == off, val, meta)
    meta_ref[...] = meta


def _mix(attn, p, x2d, ada3, w_pool, pool_scale, w_out, norm2_g, w_grp, w_exp, B, S):
    T, D = x2d.shape
    tm = min(256, S)
    tiles_per_seq = S // tm
    halo_blocks = tm // POOL_HALO
    bf = jnp.bfloat16
    w_router = jnp.concatenate(
        [w_grp, w_exp, jnp.zeros((D, LANES - N_EXPERT_GROUPS - N_EXPERTS), w_grp.dtype)], axis=1)
    row = lambda w: pl.BlockSpec((tm, w), lambda i: (i, 0))
    const = lambda shape: pl.BlockSpec(shape, lambda i: (0,) * len(shape))
    once = lambda shape: pl.BlockSpec(shape, lambda i: (0,) * len(shape), pipeline_mode=pl.Buffered(1))
    kern = functools.partial(_mix_kernel, tm=tm, tiles_per_seq=tiles_per_seq)
    return pl.pallas_call(
        kern,
        grid=(T // tm,),
        in_specs=[row(ATTN_WIDTH), row(POOL_WIDTH),
                  pl.BlockSpec((POOL_HALO, POOL_WIDTH), lambda i: (jnp.maximum(i * halo_blocks - 1, 0), 0)),
                  row(D),
                  pl.BlockSpec((1, 6, D), lambda i: (i // tiles_per_seq, 0, 0)),
                  once((POOL_GROUPS, POOL_GROUP_DIM, POOL_GROUP_DIM)), const((1, POOL_WIDTH)),
                  once((ATTN_WIDTH + POOL_WIDTH, D)), const((1, D)), once((D, LANES))],
        out_specs=[row(D), row(D), row(LANES), const((1, LANES))],
        out_shape=[jax.ShapeDtypeStruct((T, D), jnp.float32),
                   jax.ShapeDtypeStruct((T, D), jnp.float32),
                   jax.ShapeDtypeStruct((T, LANES), jnp.float32),
                   jax.ShapeDtypeStruct((1, LANES), jnp.float32)],
        scratch_shapes=[pltpu.VMEM((POOL_HALO + tm, POOL_WIDTH), jnp.float32),
                        pltpu.VMEM((1, LANES), jnp.float32),
                        pltpu.VMEM((2, D, LANES), bf)],
        compiler_params=_cparams(("arbitrary",)),
        name="mix",
    )(attn, p, p, x2d, ada3, w_pool.astype(bf), pool_scale.reshape(1, POOL_WIDTH), w_out.astype(bf),
      norm2_g.reshape(1, D), w_router)


def _dispatch_kernel(pad_start_ref, pad_len_ref, n_used_ref, pos_ref, h2_ref, xs_ref, zeros_ref, sem, *, rt, n_tiles):
    i = pl.program_id(0)

    @pl.when(i == 0)
    def _():
        zeros_ref[...] = jnp.zeros(zeros_ref.shape, zeros_ref.dtype)

        def zero_row(r):
            return pltpu.make_async_copy(zeros_ref.at[pl.ds(0, 1)], xs_ref.at[pl.ds(r, 1)], sem)

        def zero_tile(t):
            start = pl.multiple_of(t * EXPERT_TILE, EXPERT_TILE)
            return pltpu.make_async_copy(zeros_ref, xs_ref.at[pl.ds(start, EXPERT_TILE)], sem)

        def fill(e, carry):
            start = pad_start_ref[e]
            n = pad_len_ref[e]
            lax.fori_loop(0, n, lambda r, c: (zero_row(start + r).start(), c)[1], 0)
            lax.fori_loop(0, n, lambda r, c: (zero_row(0).wait(), c)[1], 0)
            return carry

        lax.fori_loop(0, N_EXPERTS, fill, 0)
        first_unused = n_used_ref[0]
        lax.fori_loop(first_unused, n_tiles, lambda t, c: (zero_tile(t).start(), c)[1], 0)
        lax.fori_loop(first_unused, n_tiles, lambda t, c: (zero_tile(0).wait(), c)[1], 0)

    def row_copy(t, slot):
        return pltpu.make_async_copy(h2_ref.at[pl.ds(i * rt + t, 1)],
                                     xs_ref.at[pl.ds(pos_ref[0, 0, 2 * t + slot], 1)], sem)

    def issue(t, carry):
        row_copy(t, 0).start(priority=0)
        row_copy(t, 1).start(priority=1)
        return carry

    def drain(t, carry):
        for _ in range(2):
            pltpu.make_async_copy(h2_ref.at[pl.ds(0, 1)], xs_ref.at[pl.ds(0, 1)], sem).wait()
        return carry

    lax.fori_loop(0, rt, issue, 0)
    lax.fori_loop(0, rt, drain, 0)


def _dispatch(h2, pos, pad_start, pad_len, n_used, n_rows):
    T, D = h2.shape
    rt = min(ROW_TILE, T)
    kern = functools.partial(_dispatch_kernel, rt=rt, n_tiles=n_rows // EXPERT_TILE)
    return pl.pallas_call(
        kern,
        grid_spec=pltpu.PrefetchScalarGridSpec(
            num_scalar_prefetch=3,
            grid=(T // rt,),
            in_specs=[pl.BlockSpec((1, 1, 2 * rt), lambda i, *_: (i, 0, 0), memory_space=pltpu.SMEM),
                      pl.BlockSpec(memory_space=pl.ANY)],
            out_specs=pl.BlockSpec(memory_space=pl.ANY),
            scratch_shapes=[pltpu.VMEM((EXPERT_TILE, D), h2.dtype), pltpu.SemaphoreType.DMA(())]),
        out_shape=jax.ShapeDtypeStruct((n_rows, D), h2.dtype),
        compiler_params=_cparams(("arbitrary",)),
        name="dispatch",
    )(pad_start, pad_len, n_used, pos.reshape(T // rt, 1, 2 * rt), h2)


def _experts_kernel(tile_expert_ref, n_used_ref, xs_ref, w1_ref, w3_ref, w2_ref, ys_ref, w1b_ref, w3b_ref, w2b_ref):
    i = pl.program_id(0)
    bf = jnp.bfloat16

    @pl.when((i == 0) | (tile_expert_ref[i] != tile_expert_ref[jnp.maximum(i - 1, 0)]))
    def _():
        w1b_ref[...] = w1_ref[0].astype(bf)
        w3b_ref[...] = w3_ref[0].astype(bf)
        w2b_ref[...] = w2_ref[0].astype(bf)

    @pl.when(i < n_used_ref[0])
    def _():
        xb = xs_ref[...].astype(bf)
        a1 = jnp.dot(xb, w1b_ref[...], preferred_element_type=jnp.float32)
        a3 = jnp.dot(xb, w3b_ref[...], preferred_element_type=jnp.float32)
        act = (a1 * (1.0 / (1.0 + jnp.exp(-a1))) * a3).astype(bf)
        ys_ref[...] = jnp.dot(act, w2b_ref[...], preferred_element_type=jnp.float32)

    @pl.when(i >= n_used_ref[0])
    def _():
        ys_ref[...] = jnp.zeros(ys_ref.shape, ys_ref.dtype)


def _experts(xs, w1, w3, w2, tile_expert, n_used):
    n_rows, D = xs.shape
    n_tiles = n_rows // EXPERT_TILE
    bf = jnp.bfloat16
    return pl.pallas_call(
        _experts_kernel,
        grid_spec=pltpu.PrefetchScalarGridSpec(
            num_scalar_prefetch=2,
            grid=(n_tiles,),
            in_specs=[pl.BlockSpec((EXPERT_TILE, D), lambda i, te, nu: (jnp.minimum(i, nu[0] - 1), 0)),
                      pl.BlockSpec((1, D, D_EXPERT), lambda i, te, nu: (te[i], 0, 0)),
                      pl.BlockSpec((1, D, D_EXPERT), lambda i, te, nu: (te[i], 0, 0)),
                      pl.BlockSpec((1, D_EXPERT, D), lambda i, te, nu: (te[i], 0, 0))],
            out_specs=pl.BlockSpec((EXPERT_TILE, D), lambda i, te, nu: (i, 0)),
            scratch_shapes=[pltpu.VMEM((D, D_EXPERT), bf), pltpu.VMEM((D, D_EXPERT), bf),
                            pltpu.VMEM((D_EXPERT, D), bf)]),
        out_shape=jax.ShapeDtypeStruct((n_rows, D), jnp.float32),
        compiler_params=_cparams(("arbitrary",)),
        name="experts",
    )(tile_expert, n_used, xs, w1, w3, w2)


def _combine_kernel(pos_ref, ys_ref, x1_ref, meta_ref, ada_ref, o_ref, rows_ref, sem, *, rt):
    def row_copy(t, slot):
        return pltpu.make_async_copy(ys_ref.at[pl.ds(pos_ref[0, 0, 2 * t + slot], 1)],
                                     rows_ref.at[slot, pl.ds(t, 1)], sem)

    def issue(t, carry):
        row_copy(t, 0).start(priority=0)
        row_copy(t, 1).start(priority=1)
        return carry

    def drain(t, carry):
        for slot in range(2):
            pltpu.make_async_copy(ys_ref.at[pl.ds(0, 1)], rows_ref.at[slot, pl.ds(0, 1)], sem).wait()
        return carry

    lax.fori_loop(0, rt, issue, 0)
    lax.fori_loop(0, rt, drain, 0)
    meta = meta_ref[...]
    y = meta[:, _META_W:_META_W + 1] * rows_ref[0] + meta[:, _META_W + 1:_META_W + 2] * rows_ref[1]
    o_ref[...] = x1_ref[...] + ada_ref[0, 5:6, :] * y


def _combine(ys, pos, x1, meta, ada3, S):
    T, D = x1.shape
    rt = min(ROW_TILE, S)
    tiles_per_seq = S // rt
    kern = functools.partial(_combine_kernel, rt=rt)
    return pl.pallas_call(
        kern,
        grid=(T // rt,),
        in_specs=[pl.BlockSpec((1, 1, 2 * rt), lambda i: (i, 0, 0), memory_space=pltpu.SMEM),
                  pl.BlockSpec(memory_space=pl.ANY),
                  pl.BlockSpec((rt, D), lambda i: (i, 0)),
                  pl.BlockSpec((rt, LANES), lambda i: (i, 0)),
                  pl.BlockSpec((1, 6, D), lambda i: (i // tiles_per_seq, 0, 0))],
        out_specs=pl.BlockSpec((rt, D), lambda i: (i, 0)),
        out_shape=jax.ShapeDtypeStruct((T, D), jnp.float32),
        scratch_shapes=[pltpu.VMEM((2, rt, D), jnp.float32), pltpu.SemaphoreType.DMA(())],
        compiler_params=_cparams(("arbitrary",)),
        name="combine",
    )(pos.reshape(T // rt, 1, 2 * rt), ys, x1, meta, ada3)


def _routing_plan(meta, counts, T):
    i32 = jnp.int32
    eid = meta[:, _META_EID:_META_EID + 2].reshape(-1).astype(i32)
    rank = meta[:, _META_RANK:_META_RANK + 2].reshape(-1).astype(i32)
    cnt = counts[0, :N_EXPERTS].astype(i32)
    padded = (cnt + EXPERT_TILE - 1) // EXPERT_TILE * EXPERT_TILE
    ends = jnp.cumsum(padded)
    starts = ends - padded
    pos = starts[eid] + rank
    n_rows = 2 * T + N_EXPERTS * EXPERT_TILE
    n_tiles = n_rows // EXPERT_TILE
    n_used = (ends[-1] // EXPERT_TILE).astype(i32)
    tile_start = jnp.minimum(jnp.arange(n_tiles, dtype=i32), n_used - 1) * EXPERT_TILE
    tile_expert = jnp.sum((ends[None, :] <= tile_start[:, None]).astype(i32), axis=1)
    return pos, starts + cnt, padded - cnt, tile_expert, n_used.reshape(1), n_rows


def kernel(x, c, w_ada, b_ada, norm1_g, w_in, q_norm_g, k_norm_g, w_pool, pool_scale,
           w_out, norm2_g, w_grp, w_exp, w1, w3, w2):
    B, S, D = x.shape
    T = B * S
    x2d = x.reshape(T, D)
    for l in range(w_ada.shape[0]):
        ada3 = _ada(c, w_ada[l], b_ada[l]).reshape(B, 6, D)
        q, kt, v, qi, kia, kib, wi, p = _inproj(x2d, ada3, norm1_g[l], w_in[l], q_norm_g[l], k_norm_g[l], B, S)
        attn = _dsa(q, qi, wi, kia, kib, kt, v, B, S)
        x1, h2, meta, counts = _mix(attn, p, x2d, ada3, w_pool[l], pool_scale[l], w_out[l], norm2_g[l],
                                    w_grp[l], w_exp[l], B, S)
        pos, pad_start, pad_len, tile_expert, n_used, n_rows = _routing_plan(meta, counts, T)
        xs = _dispatch(h2, pos, pad_start, pad_len, n_used, n_rows)
        ys = _experts(xs, w1[l], w3[l], w2[l], tile_expert, n_used)
        x2d = _combine(ys, pos, x1, meta, ada3, S)
    return x2d.reshape(B, S, D)
```
